```python
import math
import jax, jax.numpy as jnp
from jax import lax
import numpy as np

D_MODEL = 1024
BATCH = 16
SEQ = 4096
DEPTH = 2
DEC_BATCH = 32
DEC_SEQ = 16
PAST_LEN = 2048

CHUNK = 64
D_SSM = D_MODEL // 2
SSM_CH = 16
SSM_GROUPS = D_SSM // SSM_CH
SSM_STATE = 64
D_ATTN = D_MODEL - D_SSM
HEAD_DIM = 64
N_HEADS = D_ATTN // HEAD_DIM
PAST_CHUNKS = 8
BAND = PAST_CHUNKS + 1
REL_CLIP = 128
N_REL = 2 * REL_CLIP + 1
EPS = 1e-6
NEG_INF = -1e30
DT_MIN = 1e-3
DT_MAX = 1e-1
SPLITS = [D_SSM, 2 * D_SSM, 2 * D_SSM + D_ATTN, 2 * D_SSM + 2 * D_ATTN, 2 * D_SSM + 3 * D_ATTN]
D_IN = 2 * D_SSM + 4 * D_ATTN

kernel_name = "hymba_s5_chunkband_stream_step"


def _rms_norm(x, gain):
    xf = x.astype(jnp.float32)
    y = xf * lax.rsqrt(jnp.mean(xf * xf, axis=-1, keepdims=True) + EPS)
    return (y * gain.astype(jnp.float32)).astype(x.dtype)


def _cmul(xr, xi, yr, yi):
    return xr * yr - xi * yi, xr * yi + xi * yr


def _s5_discretize(a_re, a_im, b_re, b_im, log_dt):
    f32 = jnp.float32
    a_re = a_re.astype(f32)
    a_im = a_im.astype(f32)
    dt = jnp.exp(log_dt.astype(f32))[:, None]
    mag = jnp.exp(dt * a_re)
    ab_re = mag * jnp.cos(dt * a_im)
    ab_im = mag * jnp.sin(dt * a_im)
    den = a_re * a_re + a_im * a_im
    nr = ab_re - 1.0
    f_re = (nr * a_re + ab_im * a_im) / den
    f_im = (ab_im * a_re - nr * a_im) / den
    bb_re, bb_im = _cmul(f_re[..., None], f_im[..., None], b_re.astype(f32), b_im.astype(f32))
    return ab_re, ab_im, bb_re, bb_im


def _scan_combine(left, right):
    ar1, ai1, br1, bi1 = left
    ar2, ai2, br2, bi2 = right
    ar, ai = _cmul(ar2, ai2, ar1, ai1)
    pr, pi = _cmul(ar2, ai2, br1, bi1)
    return ar, ai, pr + br2, pi + bi2


def _s5_branch(u, a_re, a_im, b_re, b_im, c_re, c_im, d, log_dt, w_glu, h0_re, h0_im):
    f32 = jnp.float32
    bsz, length, _ = u.shape
    uf = u.astype(f32).reshape(bsz, length, SSM_GROUPS, SSM_CH)
    ab_re, ab_im, bb_re, bb_im = _s5_discretize(a_re, a_im, b_re, b_im, log_dt)
    bu_re = jnp.einsum('blgh,gph->lbgp', uf, bb_re)
    bu_im = jnp.einsum('blgh,gph->lbgp', uf, bb_im)
    if h0_re is not None:
        cr, ci = _cmul(ab_re, ab_im, h0_re.astype(f32), h0_im.astype(f32))
        bu_re = bu_re.at[0].add(cr)
        bu_im = bu_im.at[0].add(ci)
    a_re_t = jnp.broadcast_to(ab_re, (length, 1, SSM_GROUPS, SSM_STATE))
    a_im_t = jnp.broadcast_to(ab_im, (length, 1, SSM_GROUPS, SSM_STATE))
    _, _, h_re, h_im = lax.associative_scan(_scan_combine, (a_re_t, a_im_t, bu_re, bu_im), axis=0)
    y = (jnp.einsum('lbgp,ghp->blgh', h_re, c_re.astype(f32))
         - jnp.einsum('lbgp,ghp->blgh', h_im, c_im.astype(f32))
         + d.astype(f32) * uf)
    y = jax.nn.gelu(y.reshape(bsz, length, D_SSM)).astype(u.dtype)
    z_val, z_gate = jnp.split(jnp.einsum('ble,ef->blf', y, w_glu), 2, axis=-1)
    out = z_val * jax.nn.sigmoid(z_gate)
    return out, h_re[-1].astype(u.dtype), h_im[-1].astype(u.dtype)


def _rel_bias(rel_table, dist):
    idx = jnp.clip(dist, -REL_CLIP, REL_CLIP) + REL_CLIP
    return rel_table.astype(jnp.float32)[:, idx]


def _band_attention(q, k, v, rel_table):
    bsz, length = q.shape[:2]
    nc = length // CHUNK

    def chunks(t):
        return t.reshape(bsz, nc, CHUNK, N_HEADS, HEAD_DIM)

    pad = ((0, 0), (PAST_CHUNKS, 0), (0, 0), (0, 0), (0, 0))
    kp = jnp.pad(chunks(k), pad)
    vp = jnp.pad(chunks(v), pad)
    kb = jnp.concatenate([kp[:, j:j + nc] for j in range(BAND)], axis=2)
    vb = jnp.concatenate([vp[:, j:j + nc] for j in range(BAND)], axis=2)
    s = jnp.einsum('bnqhd,bnkhd->bnhqk', chunks(q), kb).astype(jnp.float32) * (HEAD_DIM ** -0.5)
    dist = jnp.arange(CHUNK)[:, None] + PAST_CHUNKS * CHUNK - jnp.arange(BAND * CHUNK)[None, :]
    bias = _rel_bias(rel_table, dist)
    valid = (jnp.arange(nc)[:, None] - PAST_CHUNKS + jnp.arange(BAND)[None, :]) >= 0
    valid = jnp.repeat(valid, CHUNK, axis=1)
    s = jnp.where(valid[None, :, None, None, :], s + bias[None, None], NEG_INF)
    p = jax.nn.softmax(s, axis=-1).astype(v.dtype)
    o = jnp.einsum('bnhqk,bnkhd->bnqhd', p, vb)
    return o.reshape(bsz, length, D_ATTN)


def _cached_attention(q, k, v, k_cache, v_cache, rel_table):
    bsz, length = q.shape[:2]
    rows = k_cache.shape[1]
    keys = jnp.concatenate([k_cache.astype(k.dtype), k], axis=1)
    vals = jnp.concatenate([v_cache.astype(v.dtype), v], axis=1)
    s = jnp.einsum('bqhd,bkhd->bhqk', q, keys).astype(jnp.float32) * (HEAD_DIM ** -0.5)
    dist = jnp.arange(length)[:, None] + rows - jnp.arange(rows + length)[None, :]
    s = s + _rel_bias(rel_table, dist)[None]
    p = jax.nn.softmax(s, axis=-1).astype(v.dtype)
    o = jnp.einsum('bhqk,bkhd->bqhd', p, vals)
    return o.reshape(bsz, length, D_ATTN)


def _layer(x, norm_gain, w_in, a_re, a_im, b_re, b_im, c_re, c_im, d, log_dt, w_glu,
           q_gain, k_gain, rel_table, w_out, k_cache, v_cache, h0_re, h0_im):
    bsz, length, _ = x.shape
    h = _rms_norm(x, norm_gain)
    z = jnp.einsum('bld,de->ble', h, w_in)
    u, g_s, q, k, v, g_a = jnp.split(z, SPLITS, axis=-1)
    y_s, hr, hi = _s5_branch(u, a_re, a_im, b_re, b_im, c_re, c_im, d, log_dt, w_glu, h0_re, h0_im)

    def heads(t):
        return t.reshape(bsz, length, N_HEADS, HEAD_DIM)

    q = _rms_norm(heads(q), q_gain)
    k = _rms_norm(heads(k), k_gain)
    v = heads(v)
    if k_cache is None:
        y_a = _band_attention(q, k, v, rel_table)
        rows = min(PAST_CHUNKS * CHUNK, length)
        new_k = k[:, length - rows:]
        new_v = v[:, length - rows:]
    else:
        y_a = _cached_attention(q, k, v, k_cache, v_cache, rel_table)
        new_k = k
        new_v = v
    mixed = jnp.concatenate([y_s * jax.nn.silu(g_s), y_a * jax.nn.silu(g_a)], axis=-1)
    y = x + jnp.einsum('ble,ed->bld', mixed, w_out)
    return y, new_k, new_v, hr, hi


def setup_inputs(seed: int = 0) -> dict:
    key = jax.random.key(seed)
    ks = jax.random.split(key, 24)
    f32 = jnp.float32
    kv_rows = min(PAST_CHUNKS * CHUNK, PAST_LEN)
    nrm = lambda k, shp: jax.random.normal(k, shp, f32)
    a_im_base = math.pi * jnp.arange(SSM_STATE, dtype=f32)
    return {
        "x_prompt": nrm(ks[0], (BATCH, SEQ, D_MODEL)),
        "x_sample": nrm(ks[1], (DEC_BATCH, DEC_SEQ, D_MODEL)),
        "cache_k": nrm(ks[2], (DEPTH, DEC_BATCH, kv_rows, N_HEADS, HEAD_DIM)),
        "cache_v": nrm(ks[3], (DEPTH, DEC_BATCH, kv_rows, N_HEADS, HEAD_DIM)),
        "state_ssm_re": 0.1 * nrm(ks[4], (DEPTH, DEC_BATCH, SSM_GROUPS, SSM_STATE)),
        "state_ssm_im": 0.1 * nrm(ks[5], (DEPTH, DEC_BATCH, SSM_GROUPS, SSM_STATE)),
        "norm_gain": 1.0 + 0.05 * nrm(ks[6], (DEPTH, D_MODEL)),
        "w_in": nrm(ks[7], (DEPTH, D_MODEL, D_IN)) * D_MODEL ** -0.5,
        "ssm_a_re": -0.5 + 0.01 * nrm(ks[8], (DEPTH, SSM_GROUPS, SSM_STATE)),
        "ssm_a_im": a_im_base + 0.01 * nrm(ks[9], (DEPTH, SSM_GROUPS, SSM_STATE)),
        "ssm_b_re": nrm(ks[10], (DEPTH, SSM_GROUPS, SSM_STATE, SSM_CH)) * (2 * SSM_CH) ** -0.5,
        "ssm_b_im": nrm(ks[11], (DEPTH, SSM_GROUPS, SSM_STATE, SSM_CH)) * (2 * SSM_CH) ** -0.5,
        "ssm_c_re": nrm(ks[12], (DEPTH, SSM_GROUPS, SSM_CH, SSM_STATE)) * SSM_STATE ** -0.5,
        "ssm_c_im": nrm(ks[13], (DEPTH, SSM_GROUPS, SSM_CH, SSM_STATE)) * SSM_STATE ** -0.5,
        "ssm_d": nrm(ks[14], (DEPTH, SSM_GROUPS, SSM_CH)),
        "ssm_log_dt": jax.random.uniform(ks[15], (DEPTH, SSM_GROUPS), f32, math.log(DT_MIN), math.log(DT_MAX)),
        "w_glu": nrm(ks[16], (DEPTH, D_SSM, 2 * D_SSM)) * D_SSM ** -0.5,
        "q_norm_gain": 1.0 + 0.05 * nrm(ks[17], (DEPTH, HEAD_DIM)),
        "k_norm_gain": 1.0 + 0.05 * nrm(ks[18], (DEPTH, HEAD_DIM)),
        "rel_bias": 0.1 * nrm(ks[19], (DEPTH, N_HEADS, N_REL)),
        "w_out": nrm(ks[20], (DEPTH, D_MODEL, D_MODEL)) * D_MODEL ** -0.5,
    }


def reference(x_prompt, x_sample, cache_k, cache_v, state_ssm_re, state_ssm_im, norm_gain, w_in,
              ssm_a_re, ssm_a_im, ssm_b_re, ssm_b_im, ssm_c_re, ssm_c_im, ssm_d, ssm_log_dt,
              w_glu, q_norm_gain, k_norm_gain, rel_bias, w_out):
    yp = x_prompt
    ys = x_sample
    pk, pv, pr, pi, sk, sv, sr, si = [], [], [], [], [], [], [], []
    for l in range(DEPTH):
        w = (norm_gain[l], w_in[l], ssm_a_re[l], ssm_a_im[l], ssm_b_re[l], ssm_b_im[l],
             ssm_c_re[l], ssm_c_im[l], ssm_d[l], ssm_log_dt[l], w_glu[l],
             q_norm_gain[l], k_norm_gain[l], rel_bias[l], w_out[l])
        yp, k_p, v_p, r_p, i_p = _layer(yp, *w, None, None, None, None)
        ys, k_s, v_s, r_s, i_s = _layer(ys, *w, cache_k[l], cache_v[l], state_ssm_re[l], state_ssm_im[l])
        pk.append(k_p); pv.append(v_p); pr.append(r_p); pi.append(i_p)
        sk.append(k_s); sv.append(v_s); sr.append(r_s); si.append(i_s)
    return (yp, ys, jnp.stack(pk), jnp.stack(pv), jnp.stack(pr), jnp.stack(pi),
            jnp.stack(sk), jnp.stack(sv), jnp.stack(sr), jnp.stack(si))
```

```python
import functools
import math

import jax
import jax.numpy as jnp
from jax import lax
from jax.experimental import pallas as pl
from jax.experimental.pallas import tpu as pltpu

D_MODEL = 1024
D_SSM = 512
SSM_CH = 16
SSM_GROUPS = 32
SSM_STATE = 64
D_STATE = SSM_GROUPS * SSM_STATE
D_ATTN = 512
HEAD_DIM = 64
N_HEADS = 8
CHUNK = 64
PAST_CHUNKS = 8
BAND = PAST_CHUNKS + 1
KV_ROWS = PAST_CHUNKS * CHUNK
REL_CLIP = 128
N_REL = 2 * REL_CLIP + 1
REL_PAD = 384
EPS = 1e-6
NEG_INF = -1e30
D_IN = 2 * D_SSM + 4 * D_ATTN
LOG2E = math.log2(math.e)
Q_SCALE = HEAD_DIM ** -0.5 * LOG2E

LANES = 128
BIAS_W = 640
VMEM_LIMIT = 56 * 1024 * 1024

F32 = jnp.float32
BF16 = jnp.bfloat16


def _params(n_grid):
    return pltpu.CompilerParams(dimension_semantics=("arbitrary",) * n_grid,
                                vmem_limit_bytes=VMEM_LIMIT)


def _full(shape):
    n = len(shape)
    return pl.BlockSpec(shape, lambda *_: (0,) * n)


def _discretize_kernel(are_ref, aim_ref, ldt_ref, bre_ref, bim_ref,
                       abre_ref, abim_ref, bbre_ref, bbim_ref):
    a_re = are_ref[...]
    a_im = aim_ref[...]
    dt = jnp.exp(ldt_ref[...])
    mag = jnp.exp(dt * a_re)
    ab_re = mag * jnp.cos(dt * a_im)
    ab_im = mag * jnp.sin(dt * a_im)
    den = a_re * a_re + a_im * a_im
    nr = ab_re - 1.0
    f_re = (nr * a_re + ab_im * a_im) / den
    f_im = (ab_im * a_re - nr * a_im) / den
    abre_ref[...] = ab_re
    abim_ref[...] = ab_im
    b_re = bre_ref[...]
    b_im = bim_ref[...]
    bbre_ref[...] = f_re * b_re - f_im * b_im
    bbim_ref[...] = f_re * b_im + f_im * b_re


def _discretize(a_re, a_im, b_re, b_im, log_dt):
    g, p, h = SSM_GROUPS, SSM_STATE, SSM_CH
    outs = pl.pallas_call(
        _discretize_kernel,
        out_shape=(jax.ShapeDtypeStruct((g, 1, p), F32), jax.ShapeDtypeStruct((g, 1, p), F32),
                   jax.ShapeDtypeStruct((g, h, p), F32), jax.ShapeDtypeStruct((g, h, p), F32)),
        name="s5_discretize",
    )(a_re.reshape(g, 1, p), a_im.reshape(g, 1, p), log_dt.reshape(g, 1, 1),
      jnp.swapaxes(b_re, 1, 2), jnp.swapaxes(b_im, 1, 2))
    return outs


def _block_diag(blocks):
    g, r, c = blocks.shape
    eye = jnp.eye(g, dtype=blocks.dtype)
    return (blocks[:, :, None, :] * eye[:, None, :, None]).reshape(g * r, g * c)


def _in_proj_kernel(x_ref, gain_ref, w_ref, seg_ref, qg_ref, kg_ref,
                    zs_ref, q_ref, k_ref, v_ref, sga_ref, kl_ref, vl_ref):
    x = x_ref[...]
    ms = jnp.mean(x * x, axis=-1, keepdims=True)
    h = (x * lax.rsqrt(ms + EPS) * gain_ref[...]).astype(BF16)
    z = jnp.dot(h, w_ref[...], preferred_element_type=F32)
    zs_ref[...] = z[:, :2 * D_SSM]
    seg = seg_ref[...]

    def head_norm(t, gain):
        ss = jnp.dot((t * t).astype(BF16), seg, preferred_element_type=F32)
        return t * lax.rsqrt(ss * (1.0 / HEAD_DIM) + EPS) * gain

    o = 2 * D_SSM
    qn = head_norm(z[:, o:o + D_ATTN], qg_ref[...])
    kn = head_norm(z[:, o + D_ATTN:o + 2 * D_ATTN], kg_ref[...])
    v = z[:, o + 2 * D_ATTN:o + 3 * D_ATTN]
    ga = z[:, o + 3 * D_ATTN:o + 4 * D_ATTN]
    q_ref[...] = (qn * Q_SCALE).astype(BF16)
    k_ref[...] = kn.astype(BF16)
    v_ref[...] = v.astype(BF16)
    sga_ref[...] = (ga * jax.nn.sigmoid(ga)).astype(BF16)

    @pl.when(pl.program_id(1) == pl.num_programs(1) - 1)
    def _():
        kl_ref[...] = kn
        vl_ref[...] = v


def _in_proj(x, gain, w_in, seg, qg, kg, tt):
    b, l, _ = x.shape
    nt = l // tt
    rows = min(KV_ROWS, l)
    assert rows == tt
    tok = lambda width: pl.BlockSpec((None, tt, width), lambda bi, ti: (bi, ti, 0))
    return pl.pallas_call(
        _in_proj_kernel,
        grid=(b, nt),
        in_specs=[tok(D_MODEL), _full((1, D_MODEL)), _full((D_MODEL, D_IN)), _full((D_ATTN, D_ATTN)),
                  _full((1, D_ATTN)), _full((1, D_ATTN))],
        out_specs=[pl.BlockSpec((tt, 2 * D_SSM), lambda bi, ti: (ti, bi)),
                   tok(D_ATTN), tok(D_ATTN), tok(D_ATTN), tok(D_ATTN),
                   pl.BlockSpec((None, rows, D_ATTN), lambda bi, ti: (bi, 0, 0)),
                   pl.BlockSpec((None, rows, D_ATTN), lambda bi, ti: (bi, 0, 0))],
        out_shape=(jax.ShapeDtypeStruct((l, b * 2 * D_SSM), F32),
                   jax.ShapeDtypeStruct((b, l, D_ATTN), BF16), jax.ShapeDtypeStruct((b, l, D_ATTN), BF16),
                   jax.ShapeDtypeStruct((b, l, D_ATTN), BF16), jax.ShapeDtypeStruct((b, l, D_ATTN), BF16),
                   jax.ShapeDtypeStruct((b, rows, D_ATTN), F32), jax.ShapeDtypeStruct((b, rows, D_ATTN), F32)),
        compiler_params=_params(2),
        name="in_proj",
    )(x, gain, w_in, seg, qg, kg)


S5_SLAB = 512
S5_MM_SLAB = 128


def _gelu_tanh(y):
    return 0.5 * y * (1.0 + jnp.tanh(math.sqrt(2.0 / math.pi) * (y + 0.044715 * (y * y * y))))


def _s5_kernel(zs_ref, h0re_ref, h0im_ref, abre_ref, abim_ref, bre_ref, bim_ref, cre_ref, cim_ref,
               d_ref, wglu_ref, ms_ref, hre_ref, him_ref, bure, buim, *, t_len, bsz):
    rows = t_len * bsz

    @pl.when(pl.program_id(0) == 0)
    def _():
        hre_ref[...] = h0re_ref[...]
        him_ref[...] = h0im_ref[...]

    zs = zs_ref[...].reshape(rows, 2 * D_SSM)
    u = zs[:, :D_SSM]
    gs = zs[:, D_SSM:]
    ub = u.astype(BF16)
    n_mm = D_SSM // S5_MM_SLAB
    sw = D_STATE // n_mm
    for s in range(n_mm):
        us = ub[:, s * S5_MM_SLAB:(s + 1) * S5_MM_SLAB]
        bure[:, s * sw:(s + 1) * sw] = jnp.dot(
            us, bre_ref[s * S5_MM_SLAB:(s + 1) * S5_MM_SLAB, s * sw:(s + 1) * sw], preferred_element_type=F32)
        buim[:, s * sw:(s + 1) * sw] = jnp.dot(
            us, bim_ref[s * S5_MM_SLAB:(s + 1) * S5_MM_SLAB, s * sw:(s + 1) * sw], preferred_element_type=F32)

    for s in range(D_STATE // S5_SLAB):
        cs = slice(s * S5_SLAB, (s + 1) * S5_SLAB)
        ar = jnp.broadcast_to(abre_ref[:, cs], (bsz, S5_SLAB))
        ai = jnp.broadcast_to(abim_ref[:, cs], (bsz, S5_SLAB))

        def body(t, carry, cs=cs, ar=ar, ai=ai):
            hr, hi = carry
            r0 = pl.multiple_of(t * bsz, bsz)
            nr = ar * hr - ai * hi + bure[pl.ds(r0, bsz), cs]
            ni = ar * hi + ai * hr + buim[pl.ds(r0, bsz), cs]
            bure[pl.ds(r0, bsz), cs] = nr
            buim[pl.ds(r0, bsz), cs] = ni
            return nr, ni

        hr, hi = lax.fori_loop(0, t_len, body, (hre_ref[:, cs], him_ref[:, cs]), unroll=4)
        hre_ref[:, cs] = hr
        him_ref[:, cs] = hi

    ys = []
    for s in range(n_mm):
        hr_b = bure[:, s * sw:(s + 1) * sw].astype(BF16)
        hi_b = buim[:, s * sw:(s + 1) * sw].astype(BF16)
        cols = slice(s * S5_MM_SLAB, (s + 1) * S5_MM_SLAB)
        ys.append(jnp.dot(hr_b, cre_ref[s * sw:(s + 1) * sw, cols], preferred_element_type=F32)
                  - jnp.dot(hi_b, cim_ref[s * sw:(s + 1) * sw, cols], preferred_element_type=F32))
    y = jnp.concatenate(ys, axis=1) + d_ref[...] * u
    y = _gelu_tanh(y).astype(BF16)
    zz = jnp.dot(y, wglu_ref[...], preferred_element_type=F32)
    out = zz[:, :D_SSM] * jax.nn.sigmoid(zz[:, D_SSM:]) * (gs * jax.nn.sigmoid(gs))
    ms_ref[...] = out.astype(BF16).reshape(t_len, bsz, D_SSM)


def _s5(zs_tm, h0re, h0im, abre, abim, bre, bim, cre, cim, d, wglu, t_len):
    l, b, _ = zs_tm.shape
    kern = functools.partial(_s5_kernel, t_len=t_len, bsz=b)
    return pl.pallas_call(
        kern,
        grid=(l // t_len,),
        in_specs=[pl.BlockSpec((t_len, b, 2 * D_SSM), lambda i: (i, 0, 0)),
                  _full((b, D_STATE)), _full((b, D_STATE)), _full((1, D_STATE)), _full((1, D_STATE)),
                  _full((D_SSM, D_STATE)), _full((D_SSM, D_STATE)), _full((D_STATE, D_SSM)), _full((D_STATE, D_SSM)),
                  _full((1, D_SSM)), _full((D_SSM, 2 * D_SSM))],
        out_specs=[pl.BlockSpec((t_len, b, D_SSM), lambda i: (i, 0, 0)),
                   _full((b, D_STATE)), _full((b, D_STATE))],
        out_shape=(jax.ShapeDtypeStruct((l, b, D_SSM), BF16),
                   jax.ShapeDtypeStruct((b, D_STATE), F32), jax.ShapeDtypeStruct((b, D_STATE), F32)),
        scratch_shapes=[pltpu.VMEM((t_len * b, D_STATE), F32), pltpu.VMEM((t_len * b, D_STATE), F32)],
        compiler_params=_params(1),
        name="s5_block",
    )(zs_tm, h0re, h0im, abre, abim, bre, bim, cre, cim, d, wglu)


def _build_bias(tab_ref, stage_ref, bias_ref, n_q):
    tab = tab_ref[...]
    t1 = tab.astype(BF16)
    r1 = tab - t1.astype(F32)
    t2 = r1.astype(BF16)
    t3 = (r1 - t2.astype(F32)).astype(BF16)
    r_io = lax.broadcasted_iota(jnp.int32, (REL_PAD, BIAS_W), 0)
    j_io = lax.broadcasted_iota(jnp.int32, (REL_PAD, BIAS_W), 1)

    def q_body(qi, carry):
        idx = jnp.clip(qi + KV_ROWS - j_io, -REL_CLIP, REL_CLIP) + REL_CLIP
        onehot = jnp.where(r_io == idx, 1.0, 0.0).astype(BF16)
        row = (jnp.dot(t1, onehot, preferred_element_type=F32)
               + jnp.dot(t2, onehot, preferred_element_type=F32)
               + jnp.dot(t3, onehot, preferred_element_type=F32)) * LOG2E
        r0 = pl.multiple_of(qi * N_HEADS, N_HEADS)
        for s in range(BIAS_W // LANES):
            stage_ref[s, pl.ds(r0, N_HEADS), :] = row[:, s * LANES:(s + 1) * LANES]
        return carry

    lax.fori_loop(0, n_q, q_body, 0)
    for h in range(N_HEADS):
        for s in range(BIAS_W // LANES):
            bias_ref[h, :, s * LANES:(s + 1) * LANES] = stage_ref.at[s][pl.ds(h, n_q, stride=N_HEADS), :]


def _attend(qc, k_of, v_of, bias_ref, n_keys, invalid):
    n_q = qc.shape[0]
    lane = lax.broadcasted_iota(jnp.int32, (n_q, LANES), 1)
    first = lane < HEAD_DIM
    outs = []
    for hp in range(N_HEADS // 2):
        qp = qc[:, hp * LANES:(hp + 1) * LANES]
        kb = k_of(hp)
        vb = v_of(hp)
        o_pair = None
        for hh in range(2):
            h = 2 * hp + hh
            keep = first if hh == 0 else jnp.logical_not(first)
            qh = jnp.where(keep, qp, jnp.zeros_like(qp))
            s = lax.dot_general(qh, kb, (((1,), (1,)), ((), ())), preferred_element_type=F32)
            s = s + bias_ref[h, :, :n_keys]
            if invalid is not None:
                s = jnp.where(invalid, NEG_INF, s)
            m = jnp.max(s, axis=1, keepdims=True)
            e = jnp.exp2(s - m)
            den = jnp.sum(e, axis=1, keepdims=True)
            o = jnp.dot(e.astype(BF16), vb, preferred_element_type=F32) * (1.0 / den)
            o_pair = o if hh == 0 else jnp.where(first, o_pair, o)
        outs.append(o_pair)
    return jnp.concatenate(outs, axis=1)


def _band_attn_kernel(q_ref, k_ref, v_ref, sga_ref, tab_ref, out_ref, kcat, vcat, stage, bias):
    bi = pl.program_id(0)
    ti = pl.program_id(1)
    n_band = BAND * CHUNK

    @pl.when(jnp.logical_and(bi == 0, ti == 0))
    def _():
        _build_bias(tab_ref, stage, bias, CHUNK)

    @pl.when(ti == 0)
    def _():
        kcat[0:KV_ROWS, :] = jnp.zeros((KV_ROWS, D_ATTN), BF16)
        vcat[0:KV_ROWS, :] = jnp.zeros((KV_ROWS, D_ATTN), BF16)

    @pl.when(ti > 0)
    def _():
        kcat[0:KV_ROWS, :] = kcat[KV_ROWS:2 * KV_ROWS, :]
        vcat[0:KV_ROWS, :] = vcat[KV_ROWS:2 * KV_ROWS, :]

    kcat[KV_ROWS:2 * KV_ROWS, :] = k_ref[...]
    vcat[KV_ROWS:2 * KV_ROWS, :] = v_ref[...]
    col = lax.broadcasted_iota(jnp.int32, (CHUNK, n_band), 1)

    def chunk_body(c, carry):
        r0 = pl.multiple_of(c * CHUNK, CHUNK)
        limit = jnp.where(ti == 0, (PAST_CHUNKS - c) * CHUNK, 0)
        invalid = col < limit
        qc = q_ref[pl.ds(r0, CHUNK), :]
        k_of = lambda hp: kcat[pl.ds(r0, n_band), hp * LANES:(hp + 1) * LANES]
        v_of = lambda hp: vcat[pl.ds(r0, n_band), hp * LANES:(hp + 1) * LANES]
        o = _attend(qc, k_of, v_of, bias, n_band, invalid)
        out_ref[pl.ds(r0, CHUNK), :] = (o * sga_ref[pl.ds(r0, CHUNK), :].astype(F32)).astype(BF16)
        return carry

    lax.fori_loop(0, KV_ROWS // CHUNK, chunk_body, 0)


def _band_attn(q, k, v, sga, tab):
    b, l, _ = q.shape
    tok = pl.BlockSpec((None, KV_ROWS, D_ATTN), lambda bi, ti: (bi, ti, 0))
    return pl.pallas_call(
        _band_attn_kernel,
        grid=(b, l // KV_ROWS),
        in_specs=[tok, tok, tok, tok, _full((N_HEADS, REL_PAD))],
        out_specs=tok,
        out_shape=jax.ShapeDtypeStruct((b, l, D_ATTN), BF16),
        scratch_shapes=[pltpu.VMEM((2 * KV_ROWS, D_ATTN), BF16), pltpu.VMEM((2 * KV_ROWS, D_ATTN), BF16),
                        pltpu.VMEM((BIAS_W // LANES, CHUNK * N_HEADS, LANES), F32),
                        pltpu.VMEM((N_HEADS, CHUNK, BIAS_W), F32)],
        compiler_params=_params(2),
        name="band_attn",
    )(q, k, v, sga, tab)


def _cached_attn_kernel(q_ref, k_ref, v_ref, sga_ref, kc_ref, vc_ref, tab_ref, out_ref,
                        kcat, vcat, stage, bias, *, n_new):
    n_keys = KV_ROWS + n_new

    @pl.when(pl.program_id(0) == 0)
    def _():
        _build_bias(tab_ref, stage, bias, n_new)

    kcat[0:KV_ROWS, :] = kc_ref[...].astype(BF16)
    vcat[0:KV_ROWS, :] = vc_ref[...].astype(BF16)
    kcat[KV_ROWS:n_keys, :] = k_ref[...]
    vcat[KV_ROWS:n_keys, :] = v_ref[...]
    k_of = lambda hp: kcat[:, hp * LANES:(hp + 1) * LANES]
    v_of = lambda hp: vcat[:, hp * LANES:(hp + 1) * LANES]
    o = _attend(q_ref[...], k_of, v_of, bias, n_keys, None)
    out_ref[...] = (o * sga_ref[...].astype(F32)).astype(BF16)


def _cached_attn(q, k, v, sga, k_cache, v_cache, tab):
    b, n_new, _ = q.shape
    assert k_cache.shape[1] == KV_ROWS
    new = pl.BlockSpec((None, n_new, D_ATTN), lambda bi: (bi, 0, 0))
    old = pl.BlockSpec((None, KV_ROWS, D_ATTN), lambda bi: (bi, 0, 0))
    kern = functools.partial(_cached_attn_kernel, n_new=n_new)
    return pl.pallas_call(
        kern,
        grid=(b,),
        in_specs=[new, new, new, new, old, old, _full((N_HEADS, REL_PAD))],
        out_specs=new,
        out_shape=jax.ShapeDtypeStruct((b, n_new, D_ATTN), BF16),
        scratch_shapes=[pltpu.VMEM((KV_ROWS + n_new, D_ATTN), BF16), pltpu.VMEM((KV_ROWS + n_new, D_ATTN), BF16),
                        pltpu.VMEM((BIAS_W // LANES, n_new * N_HEADS, LANES), F32),
                        pltpu.VMEM((N_HEADS, n_new, BIAS_W), F32)],
        compiler_params=_params(1),
        name="cached_attn",
    )(q, k, v, sga, k_cache.reshape(b, KV_ROWS, D_ATTN), v_cache.reshape(b, KV_ROWS, D_ATTN), tab)


def _out_proj_kernel(x_ref, ms_ref, ma_ref, w_ref, y_ref):
    y = x_ref[...]
    y = y + jnp.dot(ms_ref[...], w_ref[:D_SSM, :], preferred_element_type=F32)
    y = y + jnp.dot(ma_ref[...], w_ref[D_SSM:, :], preferred_element_type=F32)
    y_ref[...] = y


def _out_proj(x, ms_tm, ma, w_out, tt):
    b, l, _ = x.shape
    return pl.pallas_call(
        _out_proj_kernel,
        grid=(b, l // tt),
        in_specs=[pl.BlockSpec((None, tt, D_MODEL), lambda bi, ti: (bi, ti, 0)),
                  pl.BlockSpec((tt, D_SSM), lambda bi, ti: (ti, bi)),
                  pl.BlockSpec((None, tt, D_ATTN), lambda bi, ti: (bi, ti, 0)),
                  _full((D_MODEL, D_MODEL))],
        out_specs=pl.BlockSpec((None, tt, D_MODEL), lambda bi, ti: (bi, ti, 0)),
        out_shape=jax.ShapeDtypeStruct((b, l, D_MODEL), F32),
        compiler_params=_params(2),
        name="out_proj",
    )(x, ms_tm, ma, w_out)


PROMPT_TOKEN_TILE = 512
PROMPT_S5_STEPS = 32


def _layer(x, w, k_cache, v_cache, h0_re, h0_im):
    b, l, _ = x.shape
    prompt = k_cache is None
    tt = PROMPT_TOKEN_TILE if prompt else l
    zs_tm, q, k, v, sga, k_last, v_last = _in_proj(x, w["gain"], w["w_in"], w["seg"], w["qg"], w["kg"], tt)
    if prompt:
        h0_re = jnp.zeros((b, D_STATE), F32)
        h0_im = jnp.zeros((b, D_STATE), F32)
        t_len = PROMPT_S5_STEPS
    else:
        h0_re = h0_re.reshape(b, D_STATE)
        h0_im = h0_im.reshape(b, D_STATE)
        t_len = l
    ms_tm, h_re, h_im = _s5(zs_tm.reshape(l, b, 2 * D_SSM), h0_re, h0_im, w["abre"], w["abim"],
                            w["bre"], w["bim"], w["cre"], w["cim"], w["d"], w["wglu"], t_len)
    ms_tm = ms_tm.reshape(l, b * D_SSM)
    if prompt:
        ma = _band_attn(q, k, v, sga, w["tab"])
    else:
        ma = _cached_attn(q, k, v, sga, k_cache, v_cache, w["tab"])
    y = _out_proj(x, ms_tm, ma, w["w_out"], tt)
    rows = k_last.shape[1]
    shape5 = (b, rows, N_HEADS, HEAD_DIM)
    state = (b, SSM_GROUPS, SSM_STATE)
    return y, k_last.reshape(shape5), v_last.reshape(shape5), h_re.reshape(state), h_im.reshape(state)


def _layer_weights(norm_gain, w_in, a_re, a_im, b_re, b_im, c_re, c_im, d, log_dt, w_glu,
                   q_gain, k_gain, rel_table, w_out):
    ab_re, ab_im, bbt_re, bbt_im = _discretize(a_re, a_im, b_re, b_im, log_dt)
    head = jnp.arange(D_ATTN) // HEAD_DIM
    return {
        "gain": norm_gain.reshape(1, D_MODEL),
        "w_in": w_in.astype(BF16),
        "seg": (head[:, None] == head[None, :]).astype(BF16),
        "qg": jnp.tile(q_gain, N_HEADS).reshape(1, D_ATTN),
        "kg": jnp.tile(k_gain, N_HEADS).reshape(1, D_ATTN),
        "abre": ab_re.reshape(1, D_STATE),
        "abim": ab_im.reshape(1, D_STATE),
        "bre": _block_diag(bbt_re).astype(BF16),
        "bim": _block_diag(bbt_im).astype(BF16),
        "cre": _block_diag(jnp.swapaxes(c_re, 1, 2)).astype(BF16),
        "cim": _block_diag(jnp.swapaxes(c_im, 1, 2)).astype(BF16),
        "d": d.reshape(1, D_SSM),
        "wglu": w_glu.astype(BF16),
        "tab": jnp.pad(rel_table, ((0, 0), (0, REL_PAD - N_REL))),
        "w_out": w_out.astype(BF16),
    }


def kernel(x_prompt, x_sample, cache_k, cache_v, state_ssm_re, state_ssm_im, norm_gain, w_in,
           ssm_a_re, ssm_a_im, ssm_b_re, ssm_b_im, ssm_c_re, ssm_c_im, ssm_d, ssm_log_dt,
           w_glu, q_norm_gain, k_norm_gain, rel_bias, w_out):
    depth = w_in.shape[0]
    yp, ys = x_prompt, x_sample
    outs = [[] for _ in range(8)]
    for l in range(depth):
        w = _layer_weights(norm_gain[l], w_in[l], ssm_a_re[l], ssm_a_im[l], ssm_b_re[l], ssm_b_im[l],
                           ssm_c_re[l], ssm_c_im[l], ssm_d[l], ssm_log_dt[l], w_glu[l],
                           q_norm_gain[l], k_norm_gain[l], rel_bias[l], w_out[l])
        yp, k_p, v_p, r_p, i_p = _layer(yp, w, None, None, None, None)
        ys, k_s, v_s, r_s, i_s = _layer(ys, w, cache_k[l], cache_v[l], state_ssm_re[l], state_ssm_im[l])
        for lst, val in zip(outs, (k_p, v_p, r_p, i_p, k_s, v_s, r_s, i_s)):
            lst.append(val)
    return (yp, ys) + tuple(jnp.stack(lst) for lst in outs)
```

```python
import functools
import math

import jax
import jax.numpy as jnp
from jax import lax
from jax.experimental import pallas as pl
from jax.experimental.pallas import tpu as pltpu

D_MODEL = 1024
D_SSM = 512
SSM_CH = 16
SSM_GROUPS = 32
SSM_STATE = 64
D_STATE = SSM_GROUPS * SSM_STATE
D_ATTN = 512
HEAD_DIM = 64
N_HEADS = 8
CHUNK = 64
PAST_CHUNKS = 8
BAND = PAST_CHUNKS + 1
KV_ROWS = PAST_CHUNKS * CHUNK
REL_CLIP = 128
N_REL = 2 * REL_CLIP + 1
REL_PAD = 384
EPS = 1e-6
NEG_INF = -1e30
D_IN = 2 * D_SSM + 4 * D_ATTN
LOG2E = math.log2(math.e)
Q_SCALE = HEAD_DIM ** -0.5 * LOG2E

LANES = 128
BIAS_W = 640
VMEM_LIMIT = 56 * 1024 * 1024

F32 = jnp.float32
BF16 = jnp.bfloat16


def _params(n_grid):
    return pltpu.CompilerParams(dimension_semantics=("arbitrary",) * n_grid,
                                vmem_limit_bytes=VMEM_LIMIT)


def _full(shape):
    n = len(shape)
    return pl.BlockSpec(shape, lambda *_: (0,) * n)


def _discretize_kernel(are_ref, aim_ref, ldt_ref, bre_ref, bim_ref,
                       abre_ref, abim_ref, bbre_ref, bbim_ref):
    a_re = are_ref[...]
    a_im = aim_ref[...]
    dt = jnp.exp(ldt_ref[...])
    mag = jnp.exp(dt * a_re)
    ab_re = mag * jnp.cos(dt * a_im)
    ab_im = mag * jnp.sin(dt * a_im)
    den = a_re * a_re + a_im * a_im
    nr = ab_re - 1.0
    f_re = (nr * a_re + ab_im * a_im) / den
    f_im = (ab_im * a_re - nr * a_im) / den
    abre_ref[...] = ab_re
    abim_ref[...] = ab_im
    b_re = bre_ref[...]
    b_im = bim_ref[...]
    bbre_ref[...] = f_re * b_re - f_im * b_im
    bbim_ref[...] = f_re * b_im + f_im * b_re


def _discretize(a_re, a_im, b_re, b_im, log_dt):
    g, p, h = SSM_GROUPS, SSM_STATE, SSM_CH
    outs = pl.pallas_call(
        _discretize_kernel,
        out_shape=(jax.ShapeDtypeStruct((g, 1, p), F32), jax.ShapeDtypeStruct((g, 1, p), F32),
                   jax.ShapeDtypeStruct((g, h, p), F32), jax.ShapeDtypeStruct((g, h, p), F32)),
        name="s5_discretize",
    )(a_re.reshape(g, 1, p), a_im.reshape(g, 1, p), log_dt.reshape(g, 1, 1),
      jnp.swapaxes(b_re, 1, 2), jnp.swapaxes(b_im, 1, 2))
    return outs


def _block_diag(blocks):
    g, r, c = blocks.shape
    eye = jnp.eye(g, dtype=blocks.dtype)
    return (blocks[:, :, None, :] * eye[:, None, :, None]).reshape(g * r, g * c)


def _in_proj_kernel(x_ref, gain_ref, w_ref, seg_ref, qg_ref, kg_ref,
                    zs_ref, q_ref, k_ref, v_ref, sga_ref, kl_ref, vl_ref):
    x = x_ref[...]
    ms = jnp.mean(x * x, axis=-1, keepdims=True)
    h = (x * lax.rsqrt(ms + EPS) * gain_ref[...]).astype(BF16)
    z = jnp.dot(h, w_ref[...], preferred_element_type=F32)
    zs_ref[...] = z[:, :2 * D_SSM]
    seg = seg_ref[...]

    def head_norm(t, gain):
        ss = jnp.dot((t * t).astype(BF16), seg, preferred_element_type=F32)
        return t * lax.rsqrt(ss * (1.0 / HEAD_DIM) + EPS) * gain

    o = 2 * D_SSM
    qn = head_norm(z[:, o:o + D_ATTN], qg_ref[...])
    kn = head_norm(z[:, o + D_ATTN:o + 2 * D_ATTN], kg_ref[...])
    v = z[:, o + 2 * D_ATTN:o + 3 * D_ATTN]
    ga = z[:, o + 3 * D_ATTN:o + 4 * D_ATTN]
    q_ref[...] = (qn * Q_SCALE).astype(BF16)
    k_ref[...] = kn.astype(BF16)
    v_ref[...] = v.astype(BF16)
    sga_ref[...] = (ga * jax.nn.sigmoid(ga)).astype(BF16)

    @pl.when(pl.program_id(1) == pl.num_programs(1) - 1)
    def _():
        kl_ref[...] = kn
        vl_ref[...] = v


def _in_proj(x, gain, w_in, seg, qg, kg, tt):
    b, l, _ = x.shape
    nt = l // tt
    rows = min(KV_ROWS, l)
    assert rows == tt
    tok = lambda width: pl.BlockSpec((None, tt, width), lambda bi, ti: (bi, ti, 0))
    return pl.pallas_call(
        _in_proj_kernel,
        grid=(b, nt),
        in_specs=[tok(D_MODEL), _full((1, D_MODEL)), _full((D_MODEL, D_IN)), _full((D_ATTN, D_ATTN)),
                  _full((1, D_ATTN)), _full((1, D_ATTN))],
        out_specs=[pl.BlockSpec((tt, 2 * D_SSM), lambda bi, ti: (ti, bi)),
                   tok(D_ATTN), tok(D_ATTN), tok(D_ATTN), tok(D_ATTN),
                   pl.BlockSpec((None, rows, D_ATTN), lambda bi, ti: (bi, 0, 0)),
                   pl.BlockSpec((None, rows, D_ATTN), lambda bi, ti: (bi, 0, 0))],
        out_shape=(jax.ShapeDtypeStruct((l, b * 2 * D_SSM), F32),
                   jax.ShapeDtypeStruct((b, l, D_ATTN), BF16), jax.ShapeDtypeStruct((b, l, D_ATTN), BF16),
                   jax.ShapeDtypeStruct((b, l, D_ATTN), BF16), jax.ShapeDtypeStruct((b, l, D_ATTN), BF16),
                   jax.ShapeDtypeStruct((b, rows, D_ATTN), F32), jax.ShapeDtypeStruct((b, rows, D_ATTN), F32)),
        compiler_params=_params(2),
        name="in_proj",
    )(x, gain, w_in, seg, qg, kg)


S5_SLAB = 512
S5_MM_SLAB = 128


def _gelu_tanh(y):
    return 0.5 * y * (1.0 + jnp.tanh(math.sqrt(2.0 / math.pi) * (y + 0.044715 * (y * y * y))))


def _s5_kernel(zs_ref, h0re_ref, h0im_ref, abre_ref, abim_ref, bre_ref, bim_ref, cre_ref, cim_ref,
               d_ref, wglu_ref, ms_ref, hre_ref, him_ref, bure, buim, *, t_len, bsz):
    rows = t_len * bsz

    @pl.when(pl.program_id(0) == 0)
    def _():
        hre_ref[...] = h0re_ref[...]
        him_ref[...] = h0im_ref[...]

    zs = zs_ref[...].reshape(rows, 2 * D_SSM)
    u = zs[:, :D_SSM]
    gs = zs[:, D_SSM:]
    ub = u.astype(BF16)
    n_mm = D_SSM // S5_MM_SLAB
    sw = D_STATE // n_mm
    for s in range(n_mm):
        us = ub[:, s * S5_MM_SLAB:(s + 1) * S5_MM_SLAB]
        bure[:, s * sw:(s + 1) * sw] = jnp.dot(
            us, bre_ref[s * S5_MM_SLAB:(s + 1) * S5_MM_SLAB, s * sw:(s + 1) * sw], preferred_element_type=F32)
        buim[:, s * sw:(s + 1) * sw] = jnp.dot(
            us, bim_ref[s * S5_MM_SLAB:(s + 1) * S5_MM_SLAB, s * sw:(s + 1) * sw], preferred_element_type=F32)

    for s in range(D_STATE // S5_SLAB):
        cs = slice(s * S5_SLAB, (s + 1) * S5_SLAB)
        ar = jnp.broadcast_to(abre_ref[:, cs], (bsz, S5_SLAB))
        ai = jnp.broadcast_to(abim_ref[:, cs], (bsz, S5_SLAB))

        def body(t, carry, cs=cs, ar=ar, ai=ai):
            hr, hi = carry
            r0 = pl.multiple_of(t * bsz, bsz)
            nr = ar * hr - ai * hi + bure[pl.ds(r0, bsz), cs]
            ni = ar * hi + ai * hr + buim[pl.ds(r0, bsz), cs]
            bure[pl.ds(r0, bsz), cs] = nr
            buim[pl.ds(r0, bsz), cs] = ni
            return nr, ni

        hr, hi = lax.fori_loop(0, t_len, body, (hre_ref[:, cs], him_ref[:, cs]), unroll=4)
        hre_ref[:, cs] = hr
        him_ref[:, cs] = hi

    ys = []
    for s in range(n_mm):
        hr_b = bure[:, s * sw:(s + 1) * sw].astype(BF16)
        hi_b = buim[:, s * sw:(s + 1) * sw].astype(BF16)
        cols = slice(s * S5_MM_SLAB, (s + 1) * S5_MM_SLAB)
        ys.append(jnp.dot(hr_b, cre_ref[s * sw:(s + 1) * sw, cols], preferred_element_type=F32)
                  - jnp.dot(hi_b, cim_ref[s * sw:(s + 1) * sw, cols], preferred_element_type=F32))
    y = jnp.concatenate(ys, axis=1) + d_ref[...] * u
    y = _gelu_tanh(y).astype(BF16)
    zz = jnp.dot(y, wglu_ref[...], preferred_element_type=F32)
    out = zz[:, :D_SSM] * jax.nn.sigmoid(zz[:, D_SSM:]) * (gs * jax.nn.sigmoid(gs))
    ms_ref[...] = out.astype(BF16).reshape(t_len, bsz, D_SSM)


def _s5(zs_tm, h0re, h0im, abre, abim, bre, bim, cre, cim, d, wglu, t_len):
    l, b, _ = zs_tm.shape
    kern = functools.partial(_s5_kernel, t_len=t_len, bsz=b)
    return pl.pallas_call(
        kern,
        grid=(l // t_len,),
        in_specs=[pl.BlockSpec((t_len, b, 2 * D_SSM), lambda i: (i, 0, 0)),
                  _full((b, D_STATE)), _full((b, D_STATE)), _full((1, D_STATE)), _full((1, D_STATE)),
                  _full((D_SSM, D_STATE)), _full((D_SSM, D_STATE)), _full((D_STATE, D_SSM)), _full((D_STATE, D_SSM)),
                  _full((1, D_SSM)), _full((D_SSM, 2 * D_SSM))],
        out_specs=[pl.BlockSpec((t_len, b, D_SSM), lambda i: (i, 0, 0)),
                   _full((b, D_STATE)), _full((b, D_STATE))],
        out_shape=(jax.ShapeDtypeStruct((l, b, D_SSM), BF16),
                   jax.ShapeDtypeStruct((b, D_STATE), F32), jax.ShapeDtypeStruct((b, D_STATE), F32)),
        scratch_shapes=[pltpu.VMEM((t_len * b, D_STATE), F32), pltpu.VMEM((t_len * b, D_STATE), F32)],
        compiler_params=_params(1),
        name="s5_block",
    )(zs_tm, h0re, h0im, abre, abim, bre, bim, cre, cim, d, wglu)


def _build_bias(tab_ref, stage_ref, bias_ref, n_q):
    tab = tab_ref[...]
    t1 = tab.astype(BF16)
    r1 = tab - t1.astype(F32)
    t2 = r1.astype(BF16)
    t3 = (r1 - t2.astype(F32)).astype(BF16)
    r_io = lax.broadcasted_iota(jnp.int32, (REL_PAD, BIAS_W), 0)
    j_io = lax.broadcasted_iota(jnp.int32, (REL_PAD, BIAS_W), 1)

    def q_body(qi, carry):
        idx = jnp.clip(qi + KV_ROWS - j_io, -REL_CLIP, REL_CLIP) + REL_CLIP
        onehot = jnp.where(r_io == idx, 1.0, 0.0).astype(BF16)
        row = (jnp.dot(t1, onehot, preferred_element_type=F32)
               + jnp.dot(t2, onehot, preferred_element_type=F32)
               + jnp.dot(t3, onehot, preferred_element_type=F32)) * LOG2E
        r0 = pl.multiple_of(qi * N_HEADS, N_HEADS)
        for s in range(BIAS_W // LANES):
            stage_ref[s, pl.ds(r0, N_HEADS), :] = row[:, s * LANES:(s + 1) * LANES]
        return carry

    lax.fori_loop(0, n_q, q_body, 0)
    for h in range(N_HEADS):
        for s in range(BIAS_W // LANES):
            bias_ref[h, :, s * LANES:(s + 1) * LANES] = stage_ref.at[s][pl.ds(h, n_q, stride=N_HEADS), :]


def _attend(qc, k_of, v_of, bias_ref, n_keys, invalid, s_scr, p_scr, r_scr):
    n_q = qc.shape[0]
    lane = lax.broadcasted_iota(jnp.int32, (n_q, LANES), 1)
    first = lane < HEAD_DIM
    for hp in range(N_HEADS // 2):
        qp = qc[:, hp * LANES:(hp + 1) * LANES]
        kb = k_of(hp)
        for hh in range(2):
            keep = first if hh == 0 else jnp.logical_not(first)
            qh = jnp.where(keep, qp, jnp.zeros_like(qp))
            s = lax.dot_general(qh, kb, (((1,), (1,)), ((), ())), preferred_element_type=F32)
            s = s + bias_ref[2 * hp + hh, :, :n_keys]
            if invalid is not None:
                s = jnp.where(invalid, NEG_INF, s)
            s_scr[2 * hp + hh] = s
    for h in range(N_HEADS):
        s = s_scr[h]
        e = jnp.exp2(s - jnp.max(s, axis=1, keepdims=True))
        p_scr[h] = e.astype(BF16)
        r_scr[h] = jnp.broadcast_to(1.0 / jnp.sum(e, axis=1, keepdims=True), (n_q, LANES))
    outs = []
    for hp in range(N_HEADS // 2):
        vb = v_of(hp)
        o0 = jnp.dot(p_scr[2 * hp], vb, preferred_element_type=F32) * r_scr[2 * hp]
        o1 = jnp.dot(p_scr[2 * hp + 1], vb, preferred_element_type=F32) * r_scr[2 * hp + 1]
        outs.append(jnp.where(first, o0, o1))
    return jnp.concatenate(outs, axis=1)


def _attend_scratch(n_q, n_keys):
    return [pltpu.VMEM((N_HEADS, n_q, n_keys), F32), pltpu.VMEM((N_HEADS, n_q, n_keys), BF16),
            pltpu.VMEM((N_HEADS, n_q, LANES), F32)]


def _band_attn_kernel(q_ref, k_ref, v_ref, sga_ref, tab_ref, out_ref, kcat, vcat, stage, bias,
                      s_scr, p_scr, r_scr):
    bi = pl.program_id(0)
    ti = pl.program_id(1)
    n_band = BAND * CHUNK

    @pl.when(jnp.logical_and(bi == 0, ti == 0))
    def _():
        _build_bias(tab_ref, stage, bias, CHUNK)

    @pl.when(ti == 0)
    def _():
        kcat[0:KV_ROWS, :] = jnp.zeros((KV_ROWS, D_ATTN), BF16)
        vcat[0:KV_ROWS, :] = jnp.zeros((KV_ROWS, D_ATTN), BF16)

    @pl.when(ti > 0)
    def _():
        kcat[0:KV_ROWS, :] = kcat[KV_ROWS:2 * KV_ROWS, :]
        vcat[0:KV_ROWS, :] = vcat[KV_ROWS:2 * KV_ROWS, :]

    kcat[KV_ROWS:2 * KV_ROWS, :] = k_ref[...]
    vcat[KV_ROWS:2 * KV_ROWS, :] = v_ref[...]
    col = lax.broadcasted_iota(jnp.int32, (CHUNK, n_band), 1)

    def chunk_body(c, carry, *, stream_start):
        r0 = pl.multiple_of(c * CHUNK, CHUNK)
        invalid = (col < (PAST_CHUNKS - c) * CHUNK) if stream_start else None
        qc = q_ref[pl.ds(r0, CHUNK), :]
        k_of = lambda hp: kcat[pl.ds(r0, n_band), hp * LANES:(hp + 1) * LANES]
        v_of = lambda hp: vcat[pl.ds(r0, n_band), hp * LANES:(hp + 1) * LANES]
        o = _attend(qc, k_of, v_of, bias, n_band, invalid, s_scr, p_scr, r_scr)
        out_ref[pl.ds(r0, CHUNK), :] = (o * sga_ref[pl.ds(r0, CHUNK), :].astype(F32)).astype(BF16)
        return carry

    @pl.when(ti == 0)
    def _():
        lax.fori_loop(0, KV_ROWS // CHUNK, functools.partial(chunk_body, stream_start=True), 0)

    @pl.when(ti > 0)
    def _():
        lax.fori_loop(0, KV_ROWS // CHUNK, functools.partial(chunk_body, stream_start=False), 0)


def _band_attn(q, k, v, sga, tab):
    b, l, _ = q.shape
    tok = pl.BlockSpec((None, KV_ROWS, D_ATTN), lambda bi, ti: (bi, ti, 0))
    return pl.pallas_call(
        _band_attn_kernel,
        grid=(b, l // KV_ROWS),
        in_specs=[tok, tok, tok, tok, _full((N_HEADS, REL_PAD))],
        out_specs=tok,
        out_shape=jax.ShapeDtypeStruct((b, l, D_ATTN), BF16),
        scratch_shapes=[pltpu.VMEM((2 * KV_ROWS, D_ATTN), BF16), pltpu.VMEM((2 * KV_ROWS, D_ATTN), BF16),
                        pltpu.VMEM((BIAS_W // LANES, CHUNK * N_HEADS, LANES), F32),
                        pltpu.VMEM((N_HEADS, CHUNK, BIAS_W), F32),
                        *_attend_scratch(CHUNK, BAND * CHUNK)],
        compiler_params=_params(2),
        name="band_attn",
    )(q, k, v, sga, tab)


def _cached_attn_kernel(q_ref, k_ref, v_ref, sga_ref, kc_ref, vc_ref, tab_ref, out_ref,
                        kcat, vcat, stage, bias, s_scr, p_scr, r_scr, *, n_new):
    n_keys = KV_ROWS + n_new

    @pl.when(pl.program_id(0) == 0)
    def _():
        _build_bias(tab_ref, stage, bias, n_new)

    kcat[0:KV_ROWS, :] = kc_ref[...].astype(BF16)
    vcat[0:KV_ROWS, :] = vc_ref[...].astype(BF16)
    kcat[KV_ROWS:n_keys, :] = k_ref[...]
    vcat[KV_ROWS:n_keys, :] = v_ref[...]
    k_of = lambda hp: kcat[:, hp * LANES:(hp + 1) * LANES]
    v_of = lambda hp: vcat[:, hp * LANES:(hp + 1) * LANES]
    o = _attend(q_ref[...], k_of, v_of, bias, n_keys, None, s_scr, p_scr, r_scr)
    out_ref[...] = (o * sga_ref[...].astype(F32)).astype(BF16)


def _cached_attn(q, k, v, sga, k_cache, v_cache, tab):
    b, n_new, _ = q.shape
    assert k_cache.shape[1] == KV_ROWS
    new = pl.BlockSpec((None, n_new, D_ATTN), lambda bi: (bi, 0, 0))
    old = pl.BlockSpec((None, KV_ROWS, D_ATTN), lambda bi: (bi, 0, 0))
    kern = functools.partial(_cached_attn_kernel, n_new=n_new)
    return pl.pallas_call(
        kern,
        grid=(b,),
        in_specs=[new, new, new, new, old, old, _full((N_HEADS, REL_PAD))],
        out_specs=new,
        out_shape=jax.ShapeDtypeStruct((b, n_new, D_ATTN), BF16),
        scratch_shapes=[pltpu.VMEM((KV_ROWS + n_new, D_ATTN), BF16), pltpu.VMEM((KV_ROWS + n_new, D_ATTN), BF16),
                        pltpu.VMEM((BIAS_W // LANES, n_new * N_HEADS, LANES), F32),
                        pltpu.VMEM((N_HEADS, n_new, BIAS_W), F32),
                        *_attend_scratch(n_new, KV_ROWS + n_new)],
        compiler_params=_params(1),
        name="cached_attn",
    )(q, k, v, sga, k_cache.reshape(b, KV_ROWS, D_ATTN), v_cache.reshape(b, KV_ROWS, D_ATTN), tab)


def _out_proj_kernel(x_ref, ms_ref, ma_ref, w_ref, y_ref):
    y = x_ref[...]
    y = y + jnp.dot(ms_ref[...], w_ref[:D_SSM, :], preferred_element_type=F32)
    y = y + jnp.dot(ma_ref[...], w_ref[D_SSM:, :], preferred_element_type=F32)
    y_ref[...] = y


def _out_proj(x, ms_tm, ma, w_out, tt):
    b, l, _ = x.shape
    return pl.pallas_call(
        _out_proj_kernel,
        grid=(b, l // tt),
        in_specs=[pl.BlockSpec((None, tt, D_MODEL), lambda bi, ti: (bi, ti, 0)),
                  pl.BlockSpec((tt, D_SSM), lambda bi, ti: (ti, bi)),
                  pl.BlockSpec((None, tt, D_ATTN), lambda bi, ti: (bi, ti, 0)),
                  _full((D_MODEL, D_MODEL))],
        out_specs=pl.BlockSpec((None, tt, D_MODEL), lambda bi, ti: (bi, ti, 0)),
        out_shape=jax.ShapeDtypeStruct((b, l, D_MODEL), F32),
        compiler_params=_params(2),
        name="out_proj",
    )(x, ms_tm, ma, w_out)


PROMPT_TOKEN_TILE = 512
PROMPT_S5_STEPS = 32


def _layer(x, w, k_cache, v_cache, h0_re, h0_im):
    b, l, _ = x.shape
    prompt = k_cache is None
    tt = PROMPT_TOKEN_TILE if prompt else l
    zs_tm, q, k, v, sga, k_last, v_last = _in_proj(x, w["gain"], w["w_in"], w["seg"], w["qg"], w["kg"], tt)
    if prompt:
        h0_re = jnp.zeros((b, D_STATE), F32)
        h0_im = jnp.zeros((b, D_STATE), F32)
        t_len = PROMPT_S5_STEPS
    else:
        h0_re = h0_re.reshape(b, D_STATE)
        h0_im = h0_im.reshape(b, D_STATE)
        t_len = l
    ms_tm, h_re, h_im = _s5(zs_tm.reshape(l, b, 2 * D_SSM), h0_re, h0_im, w["abre"], w["abim"],
                            w["bre"], w["bim"], w["cre"], w["cim"], w["d"], w["wglu"], t_len)
    ms_tm = ms_tm.reshape(l, b * D_SSM)
    if prompt:
        ma = _band_attn(q, k, v, sga, w["tab"])
    else:
        ma = _cached_attn(q, k, v, sga, k_cache, v_cache, w["tab"])
    y = _out_proj(x, ms_tm, ma, w["w_out"], tt)
    rows = k_last.shape[1]
    shape5 = (b, rows, N_HEADS, HEAD_DIM)
    state = (b, SSM_GROUPS, SSM_STATE)
    return y, k_last.reshape(shape5), v_last.reshape(shape5), h_re.reshape(state), h_im.reshape(state)


def _layer_weights(norm_gain, w_in, a_re, a_im, b_re, b_im, c_re, c_im, d, log_dt, w_glu,
                   q_gain, k_gain, rel_table, w_out):
    ab_re, ab_im, bbt_re, bbt_im = _discretize(a_re, a_im, b_re, b_im, log_dt)
    head = jnp.arange(D_ATTN) // HEAD_DIM
    return {
        "gain": norm_gain.reshape(1, D_MODEL),
        "w_in": w_in.astype(BF16),
        "seg": (head[:, None] == head[None, :]).astype(BF16),
        "qg": jnp.tile(q_gain, N_HEADS).reshape(1, D_ATTN),
        "kg": jnp.tile(k_gain, N_HEADS).reshape(1, D_ATTN),
        "abre": ab_re.reshape(1, D_STATE),
        "abim": ab_im.reshape(1, D_STATE),
        "bre": _block_diag(bbt_re).astype(BF16),
        "bim": _block_diag(bbt_im).astype(BF16),
        "cre": _block_diag(jnp.swapaxes(c_re, 1, 2)).astype(BF16),
        "cim": _block_diag(jnp.swapaxes(c_im, 1, 2)).astype(BF16),
        "d": d.reshape(1, D_SSM),
        "wglu": w_glu.astype(BF16),
        "tab": jnp.pad(rel_table, ((0, 0), (0, REL_PAD - N_REL))),
        "w_out": w_out.astype(BF16),
    }


def kernel(x_prompt, x_sample, cache_k, cache_v, state_ssm_re, state_ssm_im, norm_gain, w_in,
           ssm_a_re, ssm_a_im, ssm_b_re, ssm_b_im, ssm_c_re, ssm_c_im, ssm_d, ssm_log_dt,
           w_glu, q_norm_gain, k_norm_gain, rel_bias, w_out):
    depth = w_in.shape[0]
    yp, ys = x_prompt, x_sample
    outs = [[] for _ in range(8)]
    for l in range(depth):
        w = _layer_weights(norm_gain[l], w_in[l], ssm_a_re[l], ssm_a_im[l], ssm_b_re[l], ssm_b_im[l],
                           ssm_c_re[l], ssm_c_im[l], ssm_d[l], ssm_log_dt[l], w_glu[l],
                           q_norm_gain[l], k_norm_gain[l], rel_bias[l], w_out[l])
        yp, k_p, v_p, r_p, i_p = _layer(yp, w, None, None, None, None)
        ys, k_s, v_s, r_s, i_s = _layer(ys, w, cache_k[l], cache_v[l], state_ssm_re[l], state_ssm_im[l])
        for lst, val in zip(outs, (k_p, v_p, r_p, i_p, k_s, v_s, r_s, i_s)):
            lst.append(val)
    return (yp, ys) + tuple(jnp.stack(lst) for lst in outs)
```

```python
import functools
import math

import jax
import jax.numpy as jnp
from jax import lax
from jax.experimental import pallas as pl
from jax.experimental.pallas import tpu as pltpu

D_MODEL = 1024
D_SSM = 512
SSM_CH = 16
SSM_GROUPS = 32
SSM_STATE = 64
D_STATE = SSM_GROUPS * SSM_STATE
D_ATTN = 512
HEAD_DIM = 64
N_HEADS = 8
CHUNK = 64
PAST_CHUNKS = 8
BAND = PAST_CHUNKS + 1
KV_ROWS = PAST_CHUNKS * CHUNK
REL_CLIP = 128
N_REL = 2 * REL_CLIP + 1
REL_PAD = 384
EPS = 1e-6
NEG_INF = -1e30
D_IN = 2 * D_SSM + 4 * D_ATTN
LOG2E = math.log2(math.e)
Q_SCALE = HEAD_DIM ** -0.5 * LOG2E

LANES = 128
BIAS_W = 640
VMEM_LIMIT = 56 * 1024 * 1024

F32 = jnp.float32
BF16 = jnp.bfloat16


def _params(n_grid):
    return pltpu.CompilerParams(dimension_semantics=("arbitrary",) * n_grid,
                                vmem_limit_bytes=VMEM_LIMIT)


def _full(shape):
    n = len(shape)
    return pl.BlockSpec(shape, lambda *_: (0,) * n)


def _discretize_kernel(are_ref, aim_ref, ldt_ref, bre_ref, bim_ref,
                       abre_ref, abim_ref, bbre_ref, bbim_ref):
    a_re = are_ref[...]
    a_im = aim_ref[...]
    dt = jnp.exp(ldt_ref[...])
    mag = jnp.exp(dt * a_re)
    ab_re = mag * jnp.cos(dt * a_im)
    ab_im = mag * jnp.sin(dt * a_im)
    den = a_re * a_re + a_im * a_im
    nr = ab_re - 1.0
    f_re = (nr * a_re + ab_im * a_im) / den
    f_im = (ab_im * a_re - nr * a_im) / den
    abre_ref[...] = ab_re
    abim_ref[...] = ab_im
    b_re = bre_ref[...]
    b_im = bim_ref[...]
    bbre_ref[...] = f_re * b_re - f_im * b_im
    bbim_ref[...] = f_re * b_im + f_im * b_re


def _discretize(a_re, a_im, b_re, b_im, log_dt):
    g, p, h = SSM_GROUPS, SSM_STATE, SSM_CH
    outs = pl.pallas_call(
        _discretize_kernel,
        out_shape=(jax.ShapeDtypeStruct((g, 1, p), F32), jax.ShapeDtypeStruct((g, 1, p), F32),
                   jax.ShapeDtypeStruct((g, h, p), F32), jax.ShapeDtypeStruct((g, h, p), F32)),
        name="s5_discretize",
    )(a_re.reshape(g, 1, p), a_im.reshape(g, 1, p), log_dt.reshape(g, 1, 1),
      jnp.swapaxes(b_re, 1, 2), jnp.swapaxes(b_im, 1, 2))
    return outs


def _block_diag(blocks):
    g, r, c = blocks.shape
    eye = jnp.eye(g, dtype=blocks.dtype)
    return (blocks[:, :, None, :] * eye[:, None, :, None]).reshape(g * r, g * c)


def _out_proj_tile(x_ref, ms_ref, ma_ref, w_ref):
    y = x_ref[...]
    y = y + jnp.dot(ms_ref[...], w_ref[:D_SSM, :], preferred_element_type=F32)
    return y + jnp.dot(ma_ref[...], w_ref[D_SSM:, :], preferred_element_type=F32)


def _in_proj_kernel(*refs, n_tiles, has_prev):
    if has_prev:
        (xp_ref, ms_ref, ma_ref, wo_ref, gain_ref, w_ref, seg_ref, qg_ref, kg_ref,
         y_ref, zs_ref, q_ref, k_ref, v_ref, sga_ref, kl_ref, vl_ref) = refs
        x = _out_proj_tile(xp_ref, ms_ref, ma_ref, wo_ref)
        y_ref[...] = x
    else:
        (x_ref, gain_ref, w_ref, seg_ref, qg_ref, kg_ref,
         zs_ref, q_ref, k_ref, v_ref, sga_ref, kl_ref, vl_ref) = refs
        x = x_ref[...]
    ms = jnp.mean(x * x, axis=-1, keepdims=True)
    h = (x * lax.rsqrt(ms + EPS) * gain_ref[...]).astype(BF16)
    z = jnp.dot(h, w_ref[...], preferred_element_type=F32)
    gsv = z[:, D_SSM:2 * D_SSM]
    zs_ref[:, :D_SSM] = z[:, :D_SSM].astype(BF16)
    zs_ref[:, D_SSM:] = (gsv * jax.nn.sigmoid(gsv)).astype(BF16)
    seg = seg_ref[...]

    def head_norm(t, gain):
        ss = jnp.dot((t * t).astype(BF16), seg, preferred_element_type=F32)
        return t * lax.rsqrt(ss * (1.0 / HEAD_DIM) + EPS) * gain

    o = 2 * D_SSM
    qn = head_norm(z[:, o:o + D_ATTN], qg_ref[...])
    kn = head_norm(z[:, o + D_ATTN:o + 2 * D_ATTN], kg_ref[...])
    v = z[:, o + 2 * D_ATTN:o + 3 * D_ATTN]
    ga = z[:, o + 3 * D_ATTN:o + 4 * D_ATTN]
    q_ref[...] = (qn * Q_SCALE).astype(BF16)
    k_ref[...] = kn.astype(BF16)
    v_ref[...] = v.astype(BF16)
    sga_ref[...] = (ga * jax.nn.sigmoid(ga)).astype(BF16)

    def write_last():
        kl_ref[...] = kn
        vl_ref[...] = v

    if n_tiles == 1:
        write_last()
    else:
        pl.when(pl.program_id(1) == n_tiles - 1)(write_last)


def _in_proj(x, gain, w_in, seg, qg, kg, tt, prev):
    b, l, _ = x.shape
    nt = l // tt
    tok = lambda width: pl.BlockSpec((None, tt, width), lambda bi, ti: (bi, ti, 0))
    last = pl.BlockSpec((None, tt, D_ATTN), lambda bi, ti: (bi, 0, 0))
    in_specs = [tok(D_MODEL), _full((1, D_MODEL)), _full((D_MODEL, D_IN)), _full((D_ATTN, D_ATTN)),
                _full((1, D_ATTN)), _full((1, D_ATTN))]
    out_specs = [tok(2 * D_SSM), tok(D_ATTN), tok(D_ATTN), tok(D_ATTN), tok(D_ATTN), last, last]
    out_shape = [jax.ShapeDtypeStruct((b, l, 2 * D_SSM), BF16),
                 jax.ShapeDtypeStruct((b, l, D_ATTN), BF16), jax.ShapeDtypeStruct((b, l, D_ATTN), BF16),
                 jax.ShapeDtypeStruct((b, l, D_ATTN), BF16), jax.ShapeDtypeStruct((b, l, D_ATTN), BF16),
                 jax.ShapeDtypeStruct((b, tt, D_ATTN), F32), jax.ShapeDtypeStruct((b, tt, D_ATTN), F32)]
    args = (x, gain, w_in, seg, qg, kg)
    if prev is not None:
        in_specs = [tok(D_MODEL), tok(D_SSM), tok(D_ATTN), _full((D_MODEL, D_MODEL))] + in_specs[1:]
        out_specs = [tok(D_MODEL)] + out_specs
        out_shape = [jax.ShapeDtypeStruct((b, l, D_MODEL), F32)] + out_shape
        args = (x,) + tuple(prev) + args[1:]
    return pl.pallas_call(
        functools.partial(_in_proj_kernel, n_tiles=nt, has_prev=prev is not None),
        grid=(b, nt),
        in_specs=in_specs,
        out_specs=out_specs,
        out_shape=out_shape,
        compiler_params=_params(2),
        name="in_proj" if prev is None else "out_in_proj",
    )(*args)


S5_MM_SLAB = 128


def _gelu_tanh(y):
    return 0.5 * y * (1.0 + jnp.tanh(math.sqrt(2.0 / math.pi) * (y + 0.044715 * (y * y * y))))


def _s5_kernel(zs_ref, h0re_ref, h0im_ref, abre_ref, abim_ref, bre_ref, bim_ref, cre_ref, cim_ref,
               d_ref, wglu_ref, ms_ref, hre_ref, him_ref, bure, buim, *, t_len, bsz):
    rows = t_len * bsz

    @pl.when(pl.program_id(0) == 0)
    def _():
        hre_ref[...] = h0re_ref[...]
        him_ref[...] = h0im_ref[...]

    zs = jnp.swapaxes(zs_ref[...], 0, 1).reshape(rows, 2 * D_SSM)
    ub = zs[:, :D_SSM]
    sgs = zs[:, D_SSM:]
    n_slab = D_SSM // S5_MM_SLAB
    sw = D_STATE // n_slab
    for s in range(n_slab):
        ch = slice(s * S5_MM_SLAB, (s + 1) * S5_MM_SLAB)
        cs = slice(s * sw, (s + 1) * sw)
        bure[:, cs] = jnp.dot(ub[:, ch], bre_ref[ch, cs], preferred_element_type=F32)
        buim[:, cs] = jnp.dot(ub[:, ch], bim_ref[ch, cs], preferred_element_type=F32)

    for s in range(n_slab):
        cs = slice(s * sw, (s + 1) * sw)
        ar = jnp.broadcast_to(abre_ref[:, cs], (bsz, sw))
        ai = jnp.broadcast_to(abim_ref[:, cs], (bsz, sw))
        hr = hre_ref[:, cs]
        hi = him_ref[:, cs]
        for t in range(t_len):
            rs = slice(t * bsz, (t + 1) * bsz)
            hr, hi = (ar * hr - ai * hi + bure[rs, cs], ar * hi + ai * hr + buim[rs, cs])
            bure[rs, cs] = hr
            buim[rs, cs] = hi
        hre_ref[:, cs] = hr
        him_ref[:, cs] = hi

    ys = []
    for s in range(n_slab):
        ch = slice(s * S5_MM_SLAB, (s + 1) * S5_MM_SLAB)
        cs = slice(s * sw, (s + 1) * sw)
        ys.append(jnp.dot(bure[:, cs].astype(BF16), cre_ref[cs, ch], preferred_element_type=F32)
                  - jnp.dot(buim[:, cs].astype(BF16), cim_ref[cs, ch], preferred_element_type=F32))
    y = jnp.concatenate(ys, axis=1) + d_ref[...] * ub.astype(F32)
    y = _gelu_tanh(y).astype(BF16)
    zz = jnp.dot(y, wglu_ref[...], preferred_element_type=F32)
    out = zz[:, :D_SSM] * jax.nn.sigmoid(zz[:, D_SSM:]) * sgs.astype(F32)
    ms_ref[...] = jnp.swapaxes(out.astype(BF16).reshape(t_len, bsz, D_SSM), 0, 1)


def _s5(zs, h0re, h0im, abre, abim, bre, bim, cre, cim, d, wglu, t_len):
    b, l, _ = zs.shape
    kern = functools.partial(_s5_kernel, t_len=t_len, bsz=b)
    return pl.pallas_call(
        kern,
        grid=(l // t_len,),
        in_specs=[pl.BlockSpec((b, t_len, 2 * D_SSM), lambda i: (0, i, 0)),
                  _full((b, D_STATE)), _full((b, D_STATE)), _full((1, D_STATE)), _full((1, D_STATE)),
                  _full((D_SSM, D_STATE)), _full((D_SSM, D_STATE)), _full((D_STATE, D_SSM)), _full((D_STATE, D_SSM)),
                  _full((1, D_SSM)), _full((D_SSM, 2 * D_SSM))],
        out_specs=[pl.BlockSpec((b, t_len, D_SSM), lambda i: (0, i, 0)),
                   _full((b, D_STATE)), _full((b, D_STATE))],
        out_shape=(jax.ShapeDtypeStruct((b, l, D_SSM), BF16),
                   jax.ShapeDtypeStruct((b, D_STATE), F32), jax.ShapeDtypeStruct((b, D_STATE), F32)),
        scratch_shapes=[pltpu.VMEM((t_len * b, D_STATE), F32), pltpu.VMEM((t_len * b, D_STATE), F32)],
        compiler_params=_params(1),
        name="s5_block",
    )(zs, h0re, h0im, abre, abim, bre, bim, cre, cim, d, wglu)


def _build_bias(tab_ref, stage_ref, bias_ref, n_rows, n_q, band_keys):
    tab = tab_ref[...]
    t1 = tab.astype(BF16)
    r1 = tab - t1.astype(F32)
    t2 = r1.astype(BF16)
    t3 = (r1 - t2.astype(F32)).astype(BF16)
    r_io = lax.broadcasted_iota(jnp.int32, (REL_PAD, BIAS_W), 0)
    j_io = lax.broadcasted_iota(jnp.int32, (REL_PAD, BIAS_W), 1)
    j_row = lax.broadcasted_iota(jnp.int32, (N_HEADS, BIAS_W), 1)

    def row_body(ri, carry):
        start = (ri // n_q) * n_q
        qi = ri - start
        idx = jnp.clip(qi + KV_ROWS + start - j_io, -REL_CLIP, REL_CLIP) + REL_CLIP
        onehot = jnp.where(r_io == idx, 1.0, 0.0).astype(BF16)
        row = (jnp.dot(t1, onehot, preferred_element_type=F32)
               + jnp.dot(t2, onehot, preferred_element_type=F32)
               + jnp.dot(t3, onehot, preferred_element_type=F32)) * LOG2E
        in_band = jnp.logical_and(j_row >= start, j_row < start + band_keys)
        row = jnp.where(in_band, row, NEG_INF)
        r0 = pl.multiple_of(ri * N_HEADS, N_HEADS)
        for s in range(BIAS_W // LANES):
            stage_ref[s, pl.ds(r0, N_HEADS), :] = row[:, s * LANES:(s + 1) * LANES]
        return carry

    lax.fori_loop(0, n_rows, row_body, 0)
    for h in range(N_HEADS):
        for s in range(BIAS_W // LANES):
            bias_ref[h, :, s * LANES:(s + 1) * LANES] = stage_ref.at[s][pl.ds(h, n_rows, stride=N_HEADS), :]


def _attend(qc, k_of, v_of, bias_ref, n_keys, invalid, s_scr, p_scr, r_scr):
    n_q = qc.shape[0]
    lane = lax.broadcasted_iota(jnp.int32, (n_q, LANES), 1)
    first = lane < HEAD_DIM
    for hp in range(N_HEADS // 2):
        qp = qc[:, hp * LANES:(hp + 1) * LANES]
        kb = k_of(hp)
        for hh in range(2):
            keep = first if hh == 0 else jnp.logical_not(first)
            qh = jnp.where(keep, qp, jnp.zeros_like(qp))
            s = lax.dot_general(qh, kb, (((1,), (1,)), ((), ())), preferred_element_type=F32)
            s = s + bias_ref[2 * hp + hh, :, :n_keys]
            if invalid is not None:
                s = jnp.where(invalid, NEG_INF, s)
            s_scr[2 * hp + hh] = s
    for h in range(N_HEADS):
        s = s_scr[h]
        e = jnp.exp2(s - jnp.max(s, axis=1, keepdims=True))
        p_scr[h] = e.astype(BF16)
        r_scr[h] = jnp.broadcast_to(1.0 / jnp.sum(e, axis=1, keepdims=True), (n_q, LANES))
    outs = []
    for hp in range(N_HEADS // 2):
        vb = v_of(hp)
        o0 = jnp.dot(p_scr[2 * hp], vb, preferred_element_type=F32) * r_scr[2 * hp]
        o1 = jnp.dot(p_scr[2 * hp + 1], vb, preferred_element_type=F32) * r_scr[2 * hp + 1]
        outs.append(jnp.where(first, o0, o1))
    return jnp.concatenate(outs, axis=1)


ATTN_ROWS = 2 * CHUNK
ATTN_WIN = (BAND + 1) * CHUNK


def _attend_scratch(n_q, n_keys):
    return [pltpu.VMEM((N_HEADS, n_q, n_keys), F32), pltpu.VMEM((N_HEADS, n_q, n_keys), BF16),
            pltpu.VMEM((N_HEADS, n_q, LANES), F32)]


def _band_attn_kernel(q_ref, k_ref, v_ref, sga_ref, tab_ref, out_ref, kcat, vcat, stage, bias,
                      s_scr, p_scr, r_scr):
    bi = pl.program_id(0)
    ti = pl.program_id(1)

    @pl.when(jnp.logical_and(bi == 0, ti == 0))
    def _():
        _build_bias(tab_ref, stage, bias, ATTN_ROWS, CHUNK, BAND * CHUNK)

    @pl.when(ti == 0)
    def _():
        kcat[0:KV_ROWS, :] = jnp.zeros((KV_ROWS, D_ATTN), BF16)
        vcat[0:KV_ROWS, :] = jnp.zeros((KV_ROWS, D_ATTN), BF16)

    @pl.when(ti > 0)
    def _():
        kcat[0:KV_ROWS, :] = kcat[KV_ROWS:2 * KV_ROWS, :]
        vcat[0:KV_ROWS, :] = vcat[KV_ROWS:2 * KV_ROWS, :]

    kcat[KV_ROWS:2 * KV_ROWS, :] = k_ref[...]
    vcat[KV_ROWS:2 * KV_ROWS, :] = v_ref[...]
    col = lax.broadcasted_iota(jnp.int32, (ATTN_ROWS, ATTN_WIN), 1)

    def block_body(it, carry, *, stream_start):
        r0 = pl.multiple_of(it * ATTN_ROWS, ATTN_ROWS)
        invalid = (col < KV_ROWS - r0) if stream_start else None
        qc = q_ref[pl.ds(r0, ATTN_ROWS), :]
        k_of = lambda hp: kcat[pl.ds(r0, ATTN_WIN), hp * LANES:(hp + 1) * LANES]
        v_of = lambda hp: vcat[pl.ds(r0, ATTN_WIN), hp * LANES:(hp + 1) * LANES]
        o = _attend(qc, k_of, v_of, bias, ATTN_WIN, invalid, s_scr, p_scr, r_scr)
        out_ref[pl.ds(r0, ATTN_ROWS), :] = (o * sga_ref[pl.ds(r0, ATTN_ROWS), :].astype(F32)).astype(BF16)
        return carry

    n_it = KV_ROWS // ATTN_ROWS

    @pl.when(ti == 0)
    def _():
        lax.fori_loop(0, n_it, functools.partial(block_body, stream_start=True), 0)

    @pl.when(ti > 0)
    def _():
        lax.fori_loop(0, n_it, functools.partial(block_body, stream_start=False), 0)


def _band_attn(q, k, v, sga, tab):
    b, l, _ = q.shape
    tok = pl.BlockSpec((None, KV_ROWS, D_ATTN), lambda bi, ti: (bi, ti, 0))
    return pl.pallas_call(
        _band_attn_kernel,
        grid=(b, l // KV_ROWS),
        in_specs=[tok, tok, tok, tok, _full((N_HEADS, REL_PAD))],
        out_specs=tok,
        out_shape=jax.ShapeDtypeStruct((b, l, D_ATTN), BF16),
        scratch_shapes=[pltpu.VMEM((2 * KV_ROWS, D_ATTN), BF16), pltpu.VMEM((2 * KV_ROWS, D_ATTN), BF16),
                        pltpu.VMEM((BIAS_W // LANES, ATTN_ROWS * N_HEADS, LANES), F32),
                        pltpu.VMEM((N_HEADS, ATTN_ROWS, BIAS_W), F32),
                        *_attend_scratch(ATTN_ROWS, ATTN_WIN)],
        compiler_params=_params(2),
        name="band_attn",
    )(q, k, v, sga, tab)


def _cached_attn_kernel(q_ref, k_ref, v_ref, sga_ref, kc_ref, vc_ref, tab_ref, out_ref,
                        kcat, vcat, stage, bias, s_scr, p_scr, r_scr, *, n_new):
    n_keys = KV_ROWS + n_new

    @pl.when(pl.program_id(0) == 0)
    def _():
        _build_bias(tab_ref, stage, bias, n_new, n_new, n_keys)

    kcat[0:KV_ROWS, :] = kc_ref[...].astype(BF16)
    vcat[0:KV_ROWS, :] = vc_ref[...].astype(BF16)
    kcat[KV_ROWS:n_keys, :] = k_ref[...]
    vcat[KV_ROWS:n_keys, :] = v_ref[...]
    k_of = lambda hp: kcat[:, hp * LANES:(hp + 1) * LANES]
    v_of = lambda hp: vcat[:, hp * LANES:(hp + 1) * LANES]
    o = _attend(q_ref[...], k_of, v_of, bias, n_keys, None, s_scr, p_scr, r_scr)
    out_ref[...] = (o * sga_ref[...].astype(F32)).astype(BF16)


def _cached_attn(q, k, v, sga, k_cache, v_cache, tab):
    b, n_new, _ = q.shape
    assert k_cache.shape[1] == KV_ROWS
    new = pl.BlockSpec((None, n_new, D_ATTN), lambda bi: (bi, 0, 0))
    old = pl.BlockSpec((None, KV_ROWS, D_ATTN), lambda bi: (bi, 0, 0))
    kern = functools.partial(_cached_attn_kernel, n_new=n_new)
    return pl.pallas_call(
        kern,
        grid=(b,),
        in_specs=[new, new, new, new, old, old, _full((N_HEADS, REL_PAD))],
        out_specs=new,
        out_shape=jax.ShapeDtypeStruct((b, n_new, D_ATTN), BF16),
        scratch_shapes=[pltpu.VMEM((KV_ROWS + n_new, D_ATTN), BF16), pltpu.VMEM((KV_ROWS + n_new, D_ATTN), BF16),
                        pltpu.VMEM((BIAS_W // LANES, n_new * N_HEADS, LANES), F32),
                        pltpu.VMEM((N_HEADS, n_new, BIAS_W), F32),
                        *_attend_scratch(n_new, KV_ROWS + n_new)],
        compiler_params=_params(1),
        name="cached_attn",
    )(q, k, v, sga, k_cache.reshape(b, KV_ROWS, D_ATTN), v_cache.reshape(b, KV_ROWS, D_ATTN), tab)


def _out_proj_kernel(x_ref, ms_ref, ma_ref, w_ref, y_ref):
    y_ref[...] = _out_proj_tile(x_ref, ms_ref, ma_ref, w_ref)


def _out_proj(x, ms, ma, w_out, tt):
    b, l, _ = x.shape
    return pl.pallas_call(
        _out_proj_kernel,
        grid=(b, l // tt),
        in_specs=[pl.BlockSpec((None, tt, D_MODEL), lambda bi, ti: (bi, ti, 0)),
                  pl.BlockSpec((None, tt, D_SSM), lambda bi, ti: (bi, ti, 0)),
                  pl.BlockSpec((None, tt, D_ATTN), lambda bi, ti: (bi, ti, 0)),
                  _full((D_MODEL, D_MODEL))],
        out_specs=pl.BlockSpec((None, tt, D_MODEL), lambda bi, ti: (bi, ti, 0)),
        out_shape=jax.ShapeDtypeStruct((b, l, D_MODEL), F32),
        compiler_params=_params(2),
        name="out_proj",
    )(x, ms, ma, w_out)


PROMPT_TOKEN_TILE = 512
PROMPT_S5_STEPS = 32


def _stream(x, weights, caches):
    b, l, _ = x.shape
    prompt = caches is None
    if prompt:
        tt = PROMPT_TOKEN_TILE
        assert min(KV_ROWS, l) == tt
        x_rows = x
        t_len = PROMPT_S5_STEPS
    else:
        tt = b * l
        x_rows = x.reshape(1, tt, D_MODEL)
        t_len = l
    split = lambda a: a.reshape(b, l, a.shape[-1])
    rows_of = lambda a: a.reshape(x_rows.shape[0], x_rows.shape[1], a.shape[-1])
    shape5 = (b, min(KV_ROWS, l), N_HEADS, HEAD_DIM)
    state = (b, SSM_GROUPS, SSM_STATE)
    outs = [[] for _ in range(4)]
    prev = None
    for li, w in enumerate(weights):
        proj = _in_proj(x_rows, w["gain"], w["w_in"], w["seg"], w["qg"], w["kg"], tt, prev)
        if prev is not None:
            x_rows, proj = proj[0], proj[1:]
        zs, q, k, v, sga = map(split, proj[:5])
        k_last, v_last = proj[5:]
        if prompt:
            h0_re = jnp.zeros((b, D_STATE), F32)
            h0_im = jnp.zeros((b, D_STATE), F32)
        else:
            h0_re = caches[li][2].reshape(b, D_STATE)
            h0_im = caches[li][3].reshape(b, D_STATE)
        ms, h_re, h_im = _s5(zs, h0_re, h0_im, w["abre"], w["abim"], w["bre"], w["bim"],
                             w["cre"], w["cim"], w["d"], w["wglu"], t_len)
        if prompt:
            ma = _band_attn(q, k, v, sga, w["tab"])
        else:
            ma = _cached_attn(q, k, v, sga, caches[li][0], caches[li][1], w["tab"])
        prev = (rows_of(ms), rows_of(ma), w["w_out"])
        for lst, val in zip(outs, (k_last.reshape(shape5), v_last.reshape(shape5),
                                   h_re.reshape(state), h_im.reshape(state))):
            lst.append(val)
    y = split(_out_proj(x_rows, *prev, tt))
    return (y,) + tuple(jnp.stack(lst) for lst in outs)


def _layer_weights(norm_gain, w_in, a_re, a_im, b_re, b_im, c_re, c_im, d, log_dt, w_glu,
                   q_gain, k_gain, rel_table, w_out):
    ab_re, ab_im, bbt_re, bbt_im = _discretize(a_re, a_im, b_re, b_im, log_dt)
    head = jnp.arange(D_ATTN) // HEAD_DIM
    return {
        "gain": norm_gain.reshape(1, D_MODEL),
        "w_in": w_in.astype(BF16),
        "seg": (head[:, None] == head[None, :]).astype(BF16),
        "qg": jnp.tile(q_gain, N_HEADS).reshape(1, D_ATTN),
        "kg": jnp.tile(k_gain, N_HEADS).reshape(1, D_ATTN),
        "abre": ab_re.reshape(1, D_STATE),
        "abim": ab_im.reshape(1, D_STATE),
        "bre": _block_diag(bbt_re).astype(BF16),
        "bim": _block_diag(bbt_im).astype(BF16),
        "cre": _block_diag(jnp.swapaxes(c_re, 1, 2)).astype(BF16),
        "cim": _block_diag(jnp.swapaxes(c_im, 1, 2)).astype(BF16),
        "d": d.reshape(1, D_SSM),
        "wglu": w_glu.astype(BF16),
        "tab": jnp.pad(rel_table, ((0, 0), (0, REL_PAD - N_REL))),
        "w_out": w_out.astype(BF16),
    }


def kernel(x_prompt, x_sample, cache_k, cache_v, state_ssm_re, state_ssm_im, norm_gain, w_in,
           ssm_a_re, ssm_a_im, ssm_b_re, ssm_b_im, ssm_c_re, ssm_c_im, ssm_d, ssm_log_dt,
           w_glu, q_norm_gain, k_norm_gain, rel_bias, w_out):
    depth = w_in.shape[0]
    weights = [_layer_weights(norm_gain[l], w_in[l], ssm_a_re[l], ssm_a_im[l], ssm_b_re[l], ssm_b_im[l],
                              ssm_c_re[l], ssm_c_im[l], ssm_d[l], ssm_log_dt[l], w_glu[l],
                              q_norm_gain[l], k_norm_gain[l], rel_bias[l], w_out[l]) for l in range(depth)]
    caches = [(cache_k[l], cache_v[l], state_ssm_re[l], state_ssm_im[l]) for l in range(depth)]
    yp, pk, pv, pr, pi = _stream(x_prompt, weights, None)
    ys, sk, sv, sr, si = _stream(x_sample, weights, caches)
    return yp, ys, pk, pv, pr, pi, sk, sv, sr, si
```

```python
import functools
import math

import jax
import jax.numpy as jnp
from jax import lax
from jax.experimental import pallas as pl
from jax.experimental.pallas import tpu as pltpu

D_MODEL = 1024
D_SSM = 512
SSM_CH = 16
SSM_GROUPS = 32
SSM_STATE = 64
D_STATE = SSM_GROUPS * SSM_STATE
D_ATTN = 512
HEAD_DIM = 64
N_HEADS = 8
CHUNK = 64
PAST_CHUNKS = 8
BAND = PAST_CHUNKS + 1
KV_ROWS = PAST_CHUNKS * CHUNK
REL_CLIP = 128
N_REL = 2 * REL_CLIP + 1
REL_PAD = 384
EPS = 1e-6
NEG_INF = -1e30
D_IN = 2 * D_SSM + 4 * D_ATTN
LOG2E = math.log2(math.e)
Q_SCALE = HEAD_DIM ** -0.5 * LOG2E

LANES = 128
BIAS_W = 640
VMEM_LIMIT = 56 * 1024 * 1024

F32 = jnp.float32
BF16 = jnp.bfloat16


def _params(n_grid):
    return pltpu.CompilerParams(dimension_semantics=("arbitrary",) * n_grid,
                                vmem_limit_bytes=VMEM_LIMIT)


def _full(shape):
    n = len(shape)
    return pl.BlockSpec(shape, lambda *_: (0,) * n)


def _discretize_kernel(are_ref, aim_ref, ldt_ref, bre_ref, bim_ref,
                       abre_ref, abim_ref, bbre_ref, bbim_ref):
    a_re = are_ref[...]
    a_im = aim_ref[...]
    dt = jnp.exp(ldt_ref[...])
    mag = jnp.exp(dt * a_re)
    ab_re = mag * jnp.cos(dt * a_im)
    ab_im = mag * jnp.sin(dt * a_im)
    den = a_re * a_re + a_im * a_im
    nr = ab_re - 1.0
    f_re = (nr * a_re + ab_im * a_im) / den
    f_im = (ab_im * a_re - nr * a_im) / den
    abre_ref[...] = ab_re
    abim_ref[...] = ab_im
    b_re = bre_ref[...]
    b_im = bim_ref[...]
    bbre_ref[...] = f_re * b_re - f_im * b_im
    bbim_ref[...] = f_re * b_im + f_im * b_re


def _discretize(a_re, a_im, b_re, b_im, log_dt):
    g, p, h = SSM_GROUPS, SSM_STATE, SSM_CH
    outs = pl.pallas_call(
        _discretize_kernel,
        out_shape=(jax.ShapeDtypeStruct((g, 1, p), F32), jax.ShapeDtypeStruct((g, 1, p), F32),
                   jax.ShapeDtypeStruct((g, h, p), F32), jax.ShapeDtypeStruct((g, h, p), F32)),
        name="s5_discretize",
    )(a_re.reshape(g, 1, p), a_im.reshape(g, 1, p), log_dt.reshape(g, 1, 1),
      jnp.swapaxes(b_re, 1, 2), jnp.swapaxes(b_im, 1, 2))
    return outs


def _block_diag(blocks):
    g, r, c = blocks.shape
    eye = jnp.eye(g, dtype=blocks.dtype)
    return (blocks[:, :, None, :] * eye[:, None, :, None]).reshape(g * r, g * c)


def _out_proj_tile(x_ref, ms_ref, ma_ref, w_ref):
    y = x_ref[...]
    y = y + jnp.dot(ms_ref[...], w_ref[:D_SSM, :], preferred_element_type=F32)
    return y + jnp.dot(ma_ref[...], w_ref[D_SSM:, :], preferred_element_type=F32)


def _in_proj_kernel(*refs, n_tiles, has_prev, n_pass):
    if has_prev:
        (xp_ref, ms_ref, ma_ref, wo_ref, gain_ref, w_ref, seg_ref, qg_ref, kg_ref,
         y_ref, zs_ref, q_ref, k_ref, v_ref, sga_ref, kl_ref, vl_ref) = refs
    else:
        (x_ref, gain_ref, w_ref, seg_ref, qg_ref, kg_ref,
         zs_ref, q_ref, k_ref, v_ref, sga_ref, kl_ref, vl_ref) = refs
    seg = seg_ref[...]

    def head_norm(t, gain):
        ss = jnp.dot((t * t).astype(BF16), seg, preferred_element_type=F32)
        return t * lax.rsqrt(ss * (1.0 / HEAD_DIM) + EPS) * gain

    rows_pass = zs_ref.shape[0] // n_pass
    for ps in range(n_pass):
        rs = slice(ps * rows_pass, (ps + 1) * rows_pass)
        if has_prev:
            x = _out_proj_tile(xp_ref.at[rs], ms_ref.at[rs], ma_ref.at[rs], wo_ref)
            y_ref[rs, :] = x
        else:
            x = x_ref[rs, :]
        ms = jnp.mean(x * x, axis=-1, keepdims=True)
        h = (x * lax.rsqrt(ms + EPS) * gain_ref[...]).astype(BF16)
        z = jnp.dot(h, w_ref[...], preferred_element_type=F32)
        gsv = z[:, D_SSM:2 * D_SSM]
        zs_ref[rs, :D_SSM] = z[:, :D_SSM].astype(BF16)
        zs_ref[rs, D_SSM:] = (gsv * jax.nn.sigmoid(gsv)).astype(BF16)
        o = 2 * D_SSM
        qn = head_norm(z[:, o:o + D_ATTN], qg_ref[...])
        kn = head_norm(z[:, o + D_ATTN:o + 2 * D_ATTN], kg_ref[...])
        v = z[:, o + 2 * D_ATTN:o + 3 * D_ATTN]
        ga = z[:, o + 3 * D_ATTN:o + 4 * D_ATTN]
        q_ref[rs, :] = (qn * Q_SCALE).astype(BF16)
        k_ref[rs, :] = kn.astype(BF16)
        v_ref[rs, :] = v.astype(BF16)
        sga_ref[rs, :] = (ga * jax.nn.sigmoid(ga)).astype(BF16)

    def write_last():
        kl_ref[...] = kn
        vl_ref[...] = v

    if n_tiles == 1:
        write_last()
    else:
        pl.when(pl.program_id(1) == n_tiles - 1)(write_last)


def _in_proj(x, gain, w_in, seg, qg, kg, tt, prev):
    b, l, _ = x.shape
    nt = l // tt
    tok = lambda width: pl.BlockSpec((None, tt, width), lambda bi, ti: (bi, ti, 0))
    n_pass = max(1, tt // PROJ_PASS_ROWS)
    last = pl.BlockSpec((None, tt // n_pass, D_ATTN), lambda bi, ti: (bi, 0, 0))
    in_specs = [tok(D_MODEL), _full((1, D_MODEL)), _full((D_MODEL, D_IN)), _full((D_ATTN, D_ATTN)),
                _full((1, D_ATTN)), _full((1, D_ATTN))]
    out_specs = [tok(2 * D_SSM), tok(D_ATTN), tok(D_ATTN), tok(D_ATTN), tok(D_ATTN), last, last]
    out_shape = [jax.ShapeDtypeStruct((b, l, 2 * D_SSM), BF16),
                 jax.ShapeDtypeStruct((b, l, D_ATTN), BF16), jax.ShapeDtypeStruct((b, l, D_ATTN), BF16),
                 jax.ShapeDtypeStruct((b, l, D_ATTN), BF16), jax.ShapeDtypeStruct((b, l, D_ATTN), BF16),
                 jax.ShapeDtypeStruct((b, tt // n_pass, D_ATTN), F32),
                 jax.ShapeDtypeStruct((b, tt // n_pass, D_ATTN), F32)]
    args = (x, gain, w_in, seg, qg, kg)
    if prev is not None:
        in_specs = [tok(D_MODEL), tok(D_SSM), tok(D_ATTN), _full((D_MODEL, D_MODEL))] + in_specs[1:]
        out_specs = [tok(D_MODEL)] + out_specs
        out_shape = [jax.ShapeDtypeStruct((b, l, D_MODEL), F32)] + out_shape
        args = (x,) + tuple(prev) + args[1:]
    return pl.pallas_call(
        functools.partial(_in_proj_kernel, n_tiles=nt, has_prev=prev is not None, n_pass=n_pass),
        grid=(b, nt),
        in_specs=in_specs,
        out_specs=out_specs,
        out_shape=out_shape,
        compiler_params=_params(2),
        name="in_proj" if prev is None else "out_in_proj",
    )(*args)


S5_MM_SLAB = 128


def _gelu_tanh(y):
    return 0.5 * y * (1.0 + jnp.tanh(math.sqrt(2.0 / math.pi) * (y + 0.044715 * (y * y * y))))


def _s5_kernel(zs_ref, h0re_ref, h0im_ref, abre_ref, abim_ref, bre_ref, bim_ref, cre_ref, cim_ref,
               d_ref, wglu_ref, ms_ref, hre_ref, him_ref, bure, buim, *, t_len, bsz):
    rows = t_len * bsz

    @pl.when(pl.program_id(0) == 0)
    def _():
        hre_ref[...] = h0re_ref[...]
        him_ref[...] = h0im_ref[...]

    zs = jnp.swapaxes(zs_ref[...], 0, 1).reshape(rows, 2 * D_SSM)
    ub = zs[:, :D_SSM]
    sgs = zs[:, D_SSM:]
    n_slab = D_SSM // S5_MM_SLAB
    sw = D_STATE // n_slab
    for s in range(n_slab):
        ch = slice(s * S5_MM_SLAB, (s + 1) * S5_MM_SLAB)
        cs = slice(s * sw, (s + 1) * sw)
        bure[:, cs] = jnp.dot(ub[:, ch], bre_ref[ch, cs], preferred_element_type=F32)
        buim[:, cs] = jnp.dot(ub[:, ch], bim_ref[ch, cs], preferred_element_type=F32)

    for s in range(n_slab):
        cs = slice(s * sw, (s + 1) * sw)
        ar = jnp.broadcast_to(abre_ref[:, cs], (bsz, sw))
        ai = jnp.broadcast_to(abim_ref[:, cs], (bsz, sw))
        hr = hre_ref[:, cs]
        hi = him_ref[:, cs]
        for t in range(t_len):
            rs = slice(t * bsz, (t + 1) * bsz)
            hr, hi = (ar * hr - ai * hi + bure[rs, cs], ar * hi + ai * hr + buim[rs, cs])
            bure[rs, cs] = hr
            buim[rs, cs] = hi
        hre_ref[:, cs] = hr
        him_ref[:, cs] = hi

    ys = []
    for s in range(n_slab):
        ch = slice(s * S5_MM_SLAB, (s + 1) * S5_MM_SLAB)
        cs = slice(s * sw, (s + 1) * sw)
        ys.append(jnp.dot(bure[:, cs].astype(BF16), cre_ref[cs, ch], preferred_element_type=F32)
                  - jnp.dot(buim[:, cs].astype(BF16), cim_ref[cs, ch], preferred_element_type=F32))
    y = jnp.concatenate(ys, axis=1) + d_ref[...] * ub.astype(F32)
    y = _gelu_tanh(y).astype(BF16)
    zz = jnp.dot(y, wglu_ref[...], preferred_element_type=F32)
    out = zz[:, :D_SSM] * jax.nn.sigmoid(zz[:, D_SSM:]) * sgs.astype(F32)
    ms_ref[...] = jnp.swapaxes(out.astype(BF16).reshape(t_len, bsz, D_SSM), 0, 1)


def _s5(zs, h0re, h0im, abre, abim, bre, bim, cre, cim, d, wglu, t_len):
    b, l, _ = zs.shape
    kern = functools.partial(_s5_kernel, t_len=t_len, bsz=b)
    return pl.pallas_call(
        kern,
        grid=(l // t_len,),
        in_specs=[pl.BlockSpec((b, t_len, 2 * D_SSM), lambda i: (0, i, 0)),
                  _full((b, D_STATE)), _full((b, D_STATE)), _full((1, D_STATE)), _full((1, D_STATE)),
                  _full((D_SSM, D_STATE)), _full((D_SSM, D_STATE)), _full((D_STATE, D_SSM)), _full((D_STATE, D_SSM)),
                  _full((1, D_SSM)), _full((D_SSM, 2 * D_SSM))],
        out_specs=[pl.BlockSpec((b, t_len, D_SSM), lambda i: (0, i, 0)),
                   _full((b, D_STATE)), _full((b, D_STATE))],
        out_shape=(jax.ShapeDtypeStruct((b, l, D_SSM), BF16),
                   jax.ShapeDtypeStruct((b, D_STATE), F32), jax.ShapeDtypeStruct((b, D_STATE), F32)),
        scratch_shapes=[pltpu.VMEM((t_len * b, D_STATE), F32), pltpu.VMEM((t_len * b, D_STATE), F32)],
        compiler_params=_params(1),
        name="s5_block",
    )(zs, h0re, h0im, abre, abim, bre, bim, cre, cim, d, wglu)


def _build_bias(tab_ref, stage_ref, bias_ref, n_rows, n_q, band_keys):
    tab = tab_ref[...]
    t1 = tab.astype(BF16)
    r1 = tab - t1.astype(F32)
    t2 = r1.astype(BF16)
    t3 = (r1 - t2.astype(F32)).astype(BF16)
    r_io = lax.broadcasted_iota(jnp.int32, (REL_PAD, BIAS_W), 0)
    j_io = lax.broadcasted_iota(jnp.int32, (REL_PAD, BIAS_W), 1)
    j_row = lax.broadcasted_iota(jnp.int32, (N_HEADS, BIAS_W), 1)

    def row_body(ri, carry):
        start = (ri // n_q) * n_q
        qi = ri - start
        idx = jnp.clip(qi + KV_ROWS + start - j_io, -REL_CLIP, REL_CLIP) + REL_CLIP
        onehot = jnp.where(r_io == idx, 1.0, 0.0).astype(BF16)
        row = (jnp.dot(t1, onehot, preferred_element_type=F32)
               + jnp.dot(t2, onehot, preferred_element_type=F32)
               + jnp.dot(t3, onehot, preferred_element_type=F32)) * LOG2E
        in_band = jnp.logical_and(j_row >= start, j_row < start + band_keys)
        row = jnp.where(in_band, row, NEG_INF)
        r0 = pl.multiple_of(ri * N_HEADS, N_HEADS)
        for s in range(BIAS_W // LANES):
            stage_ref[s, pl.ds(r0, N_HEADS), :] = row[:, s * LANES:(s + 1) * LANES]
        return carry

    lax.fori_loop(0, n_rows, row_body, 0)
    for h in range(N_HEADS):
        for s in range(BIAS_W // LANES):
            bias_ref[h, :, s * LANES:(s + 1) * LANES] = stage_ref.at[s][pl.ds(h, n_rows, stride=N_HEADS), :]


def _attend(qc, k_of, v_of, bias_ref, n_keys, invalid, s_scr, p_scr, r_scr):
    n_q = qc.shape[0]
    lane = lax.broadcasted_iota(jnp.int32, (n_q, LANES), 1)
    first = lane < HEAD_DIM
    for hp in range(N_HEADS // 2):
        qp = qc[:, hp * LANES:(hp + 1) * LANES]
        kb = k_of(hp)
        for hh in range(2):
            keep = first if hh == 0 else jnp.logical_not(first)
            qh = jnp.where(keep, qp, jnp.zeros_like(qp))
            s = lax.dot_general(qh, kb, (((1,), (1,)), ((), ())), preferred_element_type=F32)
            s = s + bias_ref[2 * hp + hh, :, :n_keys]
            if invalid is not None:
                s = jnp.where(invalid, NEG_INF, s)
            s_scr[2 * hp + hh] = s
    for h in range(N_HEADS):
        s = s_scr[h]
        e = jnp.exp2(s - jnp.max(s, axis=1, keepdims=True))
        p_scr[h] = e.astype(BF16)
        r_scr[h] = jnp.broadcast_to(1.0 / jnp.sum(e, axis=1, keepdims=True), (n_q, LANES))
    outs = []
    for hp in range(N_HEADS // 2):
        vb = v_of(hp)
        o0 = jnp.dot(p_scr[2 * hp], vb, preferred_element_type=F32) * r_scr[2 * hp]
        o1 = jnp.dot(p_scr[2 * hp + 1], vb, preferred_element_type=F32) * r_scr[2 * hp + 1]
        outs.append(jnp.where(first, o0, o1))
    return jnp.concatenate(outs, axis=1)


ATTN_ROWS = 2 * CHUNK
ATTN_WIN = (BAND + 1) * CHUNK


def _attend_scratch(n_q, n_keys):
    return [pltpu.VMEM((N_HEADS, n_q, n_keys), F32), pltpu.VMEM((N_HEADS, n_q, n_keys), BF16),
            pltpu.VMEM((N_HEADS, n_q, LANES), F32)]


def _band_attn_kernel(q_ref, k_ref, v_ref, sga_ref, tab_ref, out_ref, kprev, vprev, stage, bias,
                      s_scr, p_scr, r_scr):
    bi = pl.program_id(0)
    ti = pl.program_id(1)

    @pl.when(jnp.logical_and(bi == 0, ti == 0))
    def _():
        _build_bias(tab_ref, stage, bias, ATTN_ROWS, CHUNK, BAND * CHUNK)

    @pl.when(ti == 0)
    def _():
        kprev[...] = jnp.zeros((KV_ROWS, D_ATTN), BF16)
        vprev[...] = jnp.zeros((KV_ROWS, D_ATTN), BF16)

    col = lax.broadcasted_iota(jnp.int32, (ATTN_ROWS, ATTN_WIN), 1)

    def window(prev_ref, cur_ref, r0, hp):
        lanes = slice(hp * LANES, (hp + 1) * LANES)
        return jnp.concatenate([prev_ref[r0:, lanes], cur_ref[:r0 + ATTN_WIN - KV_ROWS, lanes]], axis=0)

    def blocks(stream_start):
        for r0 in range(0, KV_ROWS, ATTN_ROWS):
            invalid = (col < KV_ROWS - r0) if stream_start else None
            k_of = functools.partial(window, kprev, k_ref, r0)
            v_of = functools.partial(window, vprev, v_ref, r0)
            o = _attend(q_ref[r0:r0 + ATTN_ROWS, :], k_of, v_of, bias, ATTN_WIN, invalid, s_scr, p_scr, r_scr)
            out_ref[r0:r0 + ATTN_ROWS, :] = (o * sga_ref[r0:r0 + ATTN_ROWS, :].astype(F32)).astype(BF16)

    pl.when(ti == 0)(functools.partial(blocks, True))
    pl.when(ti > 0)(functools.partial(blocks, False))
    kprev[...] = k_ref[...]
    vprev[...] = v_ref[...]


def _band_attn(q, k, v, sga, tab):
    b, l, _ = q.shape
    tok = pl.BlockSpec((None, KV_ROWS, D_ATTN), lambda bi, ti: (bi, ti, 0))
    return pl.pallas_call(
        _band_attn_kernel,
        grid=(b, l // KV_ROWS),
        in_specs=[tok, tok, tok, tok, _full((N_HEADS, REL_PAD))],
        out_specs=tok,
        out_shape=jax.ShapeDtypeStruct((b, l, D_ATTN), BF16),
        scratch_shapes=[pltpu.VMEM((KV_ROWS, D_ATTN), BF16), pltpu.VMEM((KV_ROWS, D_ATTN), BF16),
                        pltpu.VMEM((BIAS_W // LANES, ATTN_ROWS * N_HEADS, LANES), F32),
                        pltpu.VMEM((N_HEADS, ATTN_ROWS, BIAS_W), F32),
                        *_attend_scratch(ATTN_ROWS, ATTN_WIN)],
        compiler_params=_params(2),
        name="band_attn",
    )(q, k, v, sga, tab)


def _cached_attn_kernel(q_ref, k_ref, v_ref, sga_ref, kct_ref, vct_ref, tab_ref, out_ref, stage, bias, *, n_new):
    n_keys = KV_ROWS + n_new

    @pl.when(pl.program_id(0) == 0)
    def _():
        _build_bias(tab_ref, stage, bias, n_new, n_new, n_keys)

    q = q_ref[...]
    k_new = k_ref[...]
    v_new = v_ref[...]
    first = lax.broadcasted_iota(jnp.int32, (n_new, LANES), 1) < HEAD_DIM
    nt_dims = (((1,), (1,)), ((), ()))
    outs = []
    for hp in range(N_HEADS // 2):
        lanes = slice(hp * LANES, (hp + 1) * LANES)
        qp = q[:, lanes]
        zero = jnp.zeros_like(qp)
        kct = kct_ref[lanes, :].astype(BF16)
        vct = vct_ref[lanes, :].astype(BF16)
        pair = []
        for hh in range(2):
            h = 2 * hp + hh
            qh = jnp.where(first, qp, zero) if hh == 0 else jnp.where(first, zero, qp)
            s_old = jnp.dot(qh, kct, preferred_element_type=F32) + bias[h, :, :KV_ROWS]
            s_new = (lax.dot_general(qh, k_new[:, lanes], nt_dims, preferred_element_type=F32)
                     + bias[h, :, KV_ROWS:n_keys])
            m = jnp.maximum(jnp.max(s_old, axis=1, keepdims=True), jnp.max(s_new, axis=1, keepdims=True))
            e_old = jnp.exp2(s_old - m)
            e_new = jnp.exp2(s_new - m)
            den = jnp.sum(e_old, axis=1, keepdims=True) + jnp.sum(e_new, axis=1, keepdims=True)
            o = (lax.dot_general(e_old.astype(BF16), vct, nt_dims, preferred_element_type=F32)
                 + jnp.dot(e_new.astype(BF16), v_new[:, lanes], preferred_element_type=F32))
            pair.append(o * (1.0 / den))
        outs.append(jnp.where(first, pair[0], pair[1]))
    o = jnp.concatenate(outs, axis=1)
    out_ref[...] = (o * sga_ref[...].astype(F32)).astype(BF16)


def _cached_attn(q, k, v, sga, k_cache, v_cache, tab):
    b, n_new, _ = q.shape
    assert k_cache.shape[1] == KV_ROWS
    as_dim_row = lambda c: jnp.transpose(c, (0, 2, 3, 1)).reshape(b, D_ATTN, KV_ROWS)
    new = pl.BlockSpec((None, n_new, D_ATTN), lambda bi: (bi, 0, 0))
    old = pl.BlockSpec((None, D_ATTN, KV_ROWS), lambda bi: (bi, 0, 0))
    kern = functools.partial(_cached_attn_kernel, n_new=n_new)
    return pl.pallas_call(
        kern,
        grid=(b,),
        in_specs=[new, new, new, new, old, old, _full((N_HEADS, REL_PAD))],
        out_specs=new,
        out_shape=jax.ShapeDtypeStruct((b, n_new, D_ATTN), BF16),
        scratch_shapes=[pltpu.VMEM((BIAS_W // LANES, n_new * N_HEADS, LANES), F32),
                        pltpu.VMEM((N_HEADS, n_new, BIAS_W), F32)],
        compiler_params=_params(1),
        name="cached_attn",
    )(q, k, v, sga, as_dim_row(k_cache), as_dim_row(v_cache), tab)


def _out_proj_kernel(x_ref, ms_ref, ma_ref, w_ref, y_ref):
    y_ref[...] = _out_proj_tile(x_ref, ms_ref, ma_ref, w_ref)


def _out_proj(x, ms, ma, w_out, tt):
    b, l, _ = x.shape
    return pl.pallas_call(
        _out_proj_kernel,
        grid=(b, l // tt),
        in_specs=[pl.BlockSpec((None, tt, D_MODEL), lambda bi, ti: (bi, ti, 0)),
                  pl.BlockSpec((None, tt, D_SSM), lambda bi, ti: (bi, ti, 0)),
                  pl.BlockSpec((None, tt, D_ATTN), lambda bi, ti: (bi, ti, 0)),
                  _full((D_MODEL, D_MODEL))],
        out_specs=pl.BlockSpec((None, tt, D_MODEL), lambda bi, ti: (bi, ti, 0)),
        out_shape=jax.ShapeDtypeStruct((b, l, D_MODEL), F32),
        compiler_params=_params(2),
        name="out_proj",
    )(x, ms, ma, w_out)


PROMPT_TOKEN_TILE = 1024
PROJ_PASS_ROWS = 512
PROMPT_S5_STEPS = 32


def _stream(x, weights, caches):
    b, l, _ = x.shape
    prompt = caches is None
    if prompt:
        tt = PROMPT_TOKEN_TILE
        assert min(KV_ROWS, l) == PROJ_PASS_ROWS and tt % PROJ_PASS_ROWS == 0
        x_rows = x
        t_len = PROMPT_S5_STEPS
    else:
        tt = b * l
        x_rows = x.reshape(1, tt, D_MODEL)
        t_len = l
    split = lambda a: a.reshape(b, l, a.shape[-1])
    rows_of = lambda a: a.reshape(x_rows.shape[0], x_rows.shape[1], a.shape[-1])
    shape5 = (b, min(KV_ROWS, l), N_HEADS, HEAD_DIM)
    state = (b, SSM_GROUPS, SSM_STATE)
    outs = [[] for _ in range(4)]
    prev = None
    for li, w in enumerate(weights):
        proj = _in_proj(x_rows, w["gain"], w["w_in"], w["seg"], w["qg"], w["kg"], tt, prev)
        if prev is not None:
            x_rows, proj = proj[0], proj[1:]
        zs, q, k, v, sga = map(split, proj[:5])
        k_last, v_last = proj[5:]
        if prompt:
            h0_re = jnp.zeros((b, D_STATE), F32)
            h0_im = jnp.zeros((b, D_STATE), F32)
        else:
            h0_re = caches[li][2].reshape(b, D_STATE)
            h0_im = caches[li][3].reshape(b, D_STATE)
        ms, h_re, h_im = _s5(zs, h0_re, h0_im, w["abre"], w["abim"], w["bre"], w["bim"],
                             w["cre"], w["cim"], w["d"], w["wglu"], t_len)
        if prompt:
            ma = _band_attn(q, k, v, sga, w["tab"])
        else:
            ma = _cached_attn(q, k, v, sga, caches[li][0], caches[li][1], w["tab"])
        prev = (rows_of(ms), rows_of(ma), w["w_out"])
        for lst, val in zip(outs, (k_last.reshape(shape5), v_last.reshape(shape5),
                                   h_re.reshape(state), h_im.reshape(state))):
            lst.append(val)
    y = split(_out_proj(x_rows, *prev, tt))
    return (y,) + tuple(jnp.stack(lst) for lst in outs)


def _layer_weights(norm_gain, w_in, a_re, a_im, b_re, b_im, c_re, c_im, d, log_dt, w_glu,
                   q_gain, k_gain, rel_table, w_out):
    ab_re, ab_im, bbt_re, bbt_im = _discretize(a_re, a_im, b_re, b_im, log_dt)
    head = jnp.arange(D_ATTN) // HEAD_DIM
    return {
        "gain": norm_gain.reshape(1, D_MODEL),
        "w_in": w_in.astype(BF16),
        "seg": (head[:, None] == head[None, :]).astype(BF16),
        "qg": jnp.tile(q_gain, N_HEADS).reshape(1, D_ATTN),
        "kg": jnp.tile(k_gain, N_HEADS).reshape(1, D_ATTN),
        "abre": ab_re.reshape(1, D_STATE),
        "abim": ab_im.reshape(1, D_STATE),
        "bre": _block_diag(bbt_re).astype(BF16),
        "bim": _block_diag(bbt_im).astype(BF16),
        "cre": _block_diag(jnp.swapaxes(c_re, 1, 2)).astype(BF16),
        "cim": _block_diag(jnp.swapaxes(c_im, 1, 2)).astype(BF16),
        "d": d.reshape(1, D_SSM),
        "wglu": w_glu.astype(BF16),
        "tab": jnp.pad(rel_table, ((0, 0), (0, REL_PAD - N_REL))),
        "w_out": w_out.astype(BF16),
    }


def kernel(x_prompt, x_sample, cache_k, cache_v, state_ssm_re, state_ssm_im, norm_gain, w_in,
           ssm_a_re, ssm_a_im, ssm_b_re, ssm_b_im, ssm_c_re, ssm_c_im, ssm_d, ssm_log_dt,
           w_glu, q_norm_gain, k_norm_gain, rel_bias, w_out):
    depth = w_in.shape[0]
    weights = [_layer_weights(norm_gain[l], w_in[l], ssm_a_re[l], ssm_a_im[l], ssm_b_re[l], ssm_b_im[l],
                              ssm_c_re[l], ssm_c_im[l], ssm_d[l], ssm_log_dt[l], w_glu[l],
                              q_norm_gain[l], k_norm_gain[l], rel_bias[l], w_out[l]) for l in range(depth)]
    caches = [(cache_k[l], cache_v[l], state_ssm_re[l], state_ssm_im[l]) for l in range(depth)]
    yp, pk, pv, pr, pi = _stream(x_prompt, weights, None)
    ys, sk, sv, sr, si = _stream(x_sample, weights, caches)
    return yp, ys, pk, pv, pr, pi, sk, sv, sr, si
```
